```python
import jax, jax.numpy as jnp
from jax import lax
import numpy as np

D_MODEL = 1024
BATCH = 8
SEQ = 2048
DEPTH = 1
DEC_BATCH = 128
DEC_SEQ = 8
PAST_LEN = 16384
PAGE_SIZE = 128

D_MIX = D_MODEL
D_A = D_MIX // 2
D_B = D_MIX - D_A
GROUP_WIDTH = 64
N_GROUPS_A = D_A // GROUP_WIDTH
N_GROUPS_B = D_B // GROUP_WIDTH
K_A = 3
K_B = 31
K_F = 3
D_FF = 2816
D_PLE = 256
EPS = 1e-6

kernel_name = "hybrid_conv_decoder_step"


def _rmsnorm(x, g):
    xf = x.astype(jnp.float32)
    y = xf * lax.rsqrt(jnp.mean(xf * xf, axis=-1, keepdims=True) + EPS)
    return (y * g.astype(jnp.float32)).astype(x.dtype)


def _layernorm(x, g, b):
    xf = x.astype(jnp.float32)
    mu = jnp.mean(xf, axis=-1, keepdims=True)
    xc = xf - mu
    var = jnp.mean(xc * xc, axis=-1, keepdims=True)
    y = xc * lax.rsqrt(var + EPS) * g.astype(jnp.float32) + b.astype(jnp.float32)
    return y.astype(x.dtype)


def _causal_dwconv(x, hist, w):
    k = w.shape[0]
    xp = jnp.concatenate([hist.astype(x.dtype), x], axis=1)
    y = lax.conv_general_dilated(
        xp, w[:, None, :].astype(x.dtype), window_strides=(1,), padding="VALID",
        dimension_numbers=("NWC", "WIO", "NWC"), feature_group_count=x.shape[-1])
    return y, xp[:, xp.shape[1] - (k - 1):]


def _layer(h, p, st_a, st_b, st_f,
           g_mix, w_in, conv_a_w, conv_b_w, conv_b_b, ln_b_g, ln_b_b, w_out,
           g_ffn, w_up, conv_f_w, conv_f_b, w_down, g_ple, w_ple, w_ple_gate):
    u = _rmsnorm(h, g_mix)
    z = jnp.einsum("btd,de->bte", u, w_in)
    b_gate, c_gate, v, glu_a, glu_g = jnp.split(
        z, [D_A, 2 * D_A, 3 * D_A, 3 * D_A + D_B], axis=-1)
    ya, new_a = _causal_dwconv(c_gate * v, st_a, conv_a_w)
    ya = b_gate * ya
    gl = glu_a * jax.nn.sigmoid(glu_g)
    yb, new_b = _causal_dwconv(gl, st_b, conv_b_w)
    yb = jax.nn.silu(_layernorm(yb + conv_b_b, ln_b_g, ln_b_b))
    h = h + jnp.einsum("bte,ed->btd", jnp.concatenate([ya, yb], axis=-1), w_out)
    u = _rmsnorm(h, g_ffn)
    up = jnp.einsum("btd,df->btf", u, w_up)
    gate, val = jnp.split(up, 2, axis=-1)
    gate, new_f = _causal_dwconv(gate, st_f, conv_f_w)
    f = jax.nn.silu(gate + conv_f_b) * val
    h = h + jnp.einsum("btf,fd->btd", f, w_down)
    gate_p = jax.nn.sigmoid(jnp.einsum("btd,de->bte", _rmsnorm(h, g_ple), w_ple_gate))
    h = h + jnp.einsum("btp,pd->btd", p, w_ple) * gate_p
    return h, new_a, new_b, new_f


def setup_inputs(seed: int = 0) -> dict:
    key = jax.random.key(seed)
    ks = jax.random.split(key, 24)
    n = lambda k, shape, s: jax.random.normal(k, shape, jnp.float32) * s
    L = DEPTH
    return {
        "x_prompt": n(ks[0], (BATCH, SEQ, D_MODEL), 1.0),
        "x_sample": n(ks[1], (DEC_BATCH, DEC_SEQ, D_MODEL), 1.0),
        "p_prompt": n(ks[2], (L, BATCH, SEQ, D_PLE), 1.0),
        "p_sample": n(ks[3], (L, DEC_BATCH, DEC_SEQ, D_PLE), 1.0),
        "state_conv_a": n(ks[4], (L, DEC_BATCH, K_A - 1, D_A), 1.0),
        "state_conv_b": n(ks[5], (L, DEC_BATCH, K_B - 1, D_B), 0.5),
        "state_ffn_conv": n(ks[6], (L, DEC_BATCH, K_F - 1, D_FF), 1.0),
        "g_mix": 1.0 + n(ks[7], (L, D_MODEL), 0.02),
        "w_in": n(ks[8], (L, D_MODEL, 3 * D_A + 2 * D_B), D_MODEL ** -0.5),
        "conv_a_w": n(ks[9], (L, K_A, D_A), K_A ** -0.5),
        "conv_b_w": n(ks[10], (L, K_B, D_B), K_B ** -0.5),
        "conv_b_b": n(ks[11], (L, D_B), 0.02),
        "ln_b_g": 1.0 + n(ks[12], (L, D_B), 0.02),
        "ln_b_b": n(ks[13], (L, D_B), 0.02),
        "w_out": n(ks[14], (L, D_MIX, D_MODEL), D_MIX ** -0.5),
        "g_ffn": 1.0 + n(ks[15], (L, D_MODEL), 0.02),
        "w_up": n(ks[16], (L, D_MODEL, 2 * D_FF), D_MODEL ** -0.5),
        "conv_f_w": n(ks[17], (L, K_F, D_FF), K_F ** -0.5),
        "conv_f_b": n(ks[18], (L, D_FF), 0.02),
        "w_down": n(ks[19], (L, D_FF, D_MODEL), D_FF ** -0.5),
        "g_ple": 1.0 + n(ks[20], (L, D_MODEL), 0.02),
        "w_ple": n(ks[21], (L, D_PLE, D_MODEL), D_PLE ** -0.5),
        "w_ple_gate": n(ks[22], (L, D_MODEL, D_MODEL), D_MODEL ** -0.5),
        "g_final": 1.0 + n(ks[23], (D_MODEL,), 0.02),
    }


def reference(x_prompt, x_sample, p_prompt, p_sample, state_conv_a, state_conv_b, state_ffn_conv,
              g_mix, w_in, conv_a_w, conv_b_w, conv_b_b, ln_b_g, ln_b_b, w_out,
              g_ffn, w_up, conv_f_w, conv_f_b, w_down, g_ple, w_ple, w_ple_gate, g_final):
    hp, hs = x_prompt, x_sample
    nb = x_prompt.shape[0]
    dt = x_prompt.dtype
    pa, pb, pf, sa, sb, sf = [], [], [], [], [], []
    for i in range(DEPTH):
        w = (g_mix[i], w_in[i], conv_a_w[i], conv_b_w[i], conv_b_b[i], ln_b_g[i], ln_b_b[i],
             w_out[i], g_ffn[i], w_up[i], conv_f_w[i], conv_f_b[i], w_down[i],
             g_ple[i], w_ple[i], w_ple_gate[i])
        hp, a1, b1, f1 = _layer(hp, p_prompt[i],
                                jnp.zeros((nb, K_A - 1, D_A), dt),
                                jnp.zeros((nb, K_B - 1, D_B), dt),
                                jnp.zeros((nb, K_F - 1, D_FF), dt), *w)
        hs, a2, b2, f2 = _layer(hs, p_sample[i], state_conv_a[i], state_conv_b[i],
                                state_ffn_conv[i], *w)
        pa.append(a1); pb.append(b1); pf.append(f1)
        sa.append(a2); sb.append(b2); sf.append(f2)
    y_prompt = _rmsnorm(hp, g_final)
    y_sample = _rmsnorm(hs, g_final)
    return (y_prompt, y_sample,
            jnp.stack(pa), jnp.stack(pb), jnp.stack(pf),
            jnp.stack(sa), jnp.stack(sb), jnp.stack(sf))
```

```python
import functools

import jax
import jax.numpy as jnp
from jax import lax
from jax.experimental import pallas as pl
from jax.experimental.pallas import tpu as pltpu

D_MODEL = 1024
D_A = 512
D_B = 512
D_FF = 2816
D_PLE = 256
K_A = 3
K_B = 31
K_F = 3
EPS = 1e-6
D_IN = 3 * D_A + 2 * D_B

LANES = 128
SUBLANES = 8
VMEM_LIMIT_BYTES = 58 * 1024 * 1024


def _rmsnorm(x, g):
    return x * lax.rsqrt(jnp.mean(x * x, axis=-1, keepdims=True) + EPS) * g


def _layernorm(x, g, b):
    mu = jnp.mean(x, axis=-1, keepdims=True)
    xc = x - mu
    var = jnp.mean(xc * xc, axis=-1, keepdims=True)
    return xc * lax.rsqrt(var + EPS) * g + b


def _dot(a, w_ref):
    return jnp.dot(a.astype(jnp.bfloat16), w_ref[...],
                   preferred_element_type=jnp.float32)


def _rows_from_view(ref, width, n_steps):
    return jnp.concatenate(
        [ref[:, t * width:(t + 1) * width] for t in range(n_steps)], axis=0)


def _rows_to_view(ref, rows, width, n_steps, seqs):
    for t in range(n_steps):
        ref[:, t * width:(t + 1) * width] = rows[t * seqs:(t + 1) * seqs, :]


def _conv_short(x_ref, w_ref, seqs, tm):
    return (w_ref[0:1, :] * x_ref[0:tm, :]
            + w_ref[1:2, :] * x_ref[seqs:seqs + tm, :]
            + w_ref[2:3, :] * x_ref[2 * seqs:2 * seqs + tm, :])


def _conv_long(x_ref, w_ref, out_ref, seqs, tm):
    n_chunks = tm // SUBLANES
    for j in range(D_B // LANES):
        lanes = slice(j * LANES, (j + 1) * LANES)
        taps = [jnp.broadcast_to(w_ref[k:k + 1, lanes], (SUBLANES, LANES))
                for k in range(K_B)]

        def chunk(c, carry, lanes=lanes, taps=taps):
            r0 = pl.multiple_of(c * SUBLANES, SUBLANES)
            acc = taps[0] * x_ref[pl.ds(r0, SUBLANES), lanes]
            for k in range(1, K_B):
                acc = acc + taps[k] * x_ref[pl.ds(r0 + k * seqs, SUBLANES), lanes]
            out_ref[pl.ds(r0, SUBLANES), lanes] = acc
            return carry

        lax.fori_loop(0, n_chunks, chunk, 0, unroll=2)


def _layer_body(*refs, seqs, steps, carry_history):
    if carry_history:
        x_ref, p_ref = refs[:2]
        rest = refs[2:]
    else:
        x_ref, p_ref, sa_ref, sb_ref, sf_ref = refs[:5]
        rest = refs[5:]
    (g_mix, w_in, conv_a_w, conv_b_w, conv_b_b, ln_b_g, ln_b_b, w_out,
     g_ffn, w_up, conv_f_w, conv_f_b, w_down, g_ple, w_ple, w_ple_gate, g_final,
     y_ref, na_ref, nb_ref, nf_ref,
     xa_ref, xb_ref, xf_ref, yb_ref) = rest

    tm = seqs * steps
    ha, hb, hf = (K_A - 1) * seqs, (K_B - 1) * seqs, (K_F - 1) * seqs

    if carry_history:
        @pl.when(pl.program_id(0) == 0)
        def _():
            xa_ref[0:ha, :] = jnp.zeros((ha, D_A), jnp.float32)
            xb_ref[0:hb, :] = jnp.zeros((hb, D_B), jnp.float32)
            xf_ref[0:hf, :] = jnp.zeros((hf, D_FF), jnp.float32)
    else:
        xa_ref[0:ha, :] = _rows_from_view(sa_ref, D_A, K_A - 1)
        xb_ref[0:hb, :] = _rows_from_view(sb_ref, D_B, K_B - 1)
        xf_ref[0:hf, :] = _rows_from_view(sf_ref, D_FF, K_F - 1)

    h = _rows_from_view(x_ref, D_MODEL, steps)

    z = _dot(_rmsnorm(h, g_mix[...]), w_in)
    xa_ref[ha:ha + tm, :] = z[:, D_A:2 * D_A] * z[:, 2 * D_A:3 * D_A]
    ya = z[:, 0:D_A] * _conv_short(xa_ref, conv_a_w, seqs, tm)
    xb_ref[hb:hb + tm, :] = (z[:, 3 * D_A:3 * D_A + D_B]
                             * jax.nn.sigmoid(z[:, 3 * D_A + D_B:D_IN]))
    _conv_long(xb_ref, conv_b_w, yb_ref, seqs, tm)
    yb = jax.nn.silu(_layernorm(yb_ref[...] + conv_b_b[...], ln_b_g[...], ln_b_b[...]))
    h = h + _dot(jnp.concatenate([ya, yb], axis=-1), w_out)

    up = _dot(_rmsnorm(h, g_ffn[...]), w_up)
    xf_ref[hf:hf + tm, :] = up[:, 0:D_FF]
    gate = _conv_short(xf_ref, conv_f_w, seqs, tm) + conv_f_b[...]
    h = h + _dot(jax.nn.silu(gate) * up[:, D_FF:2 * D_FF], w_down)

    gate_p = jax.nn.sigmoid(_dot(_rmsnorm(h, g_ple[...]), w_ple_gate))
    h = h + _dot(_rows_from_view(p_ref, D_PLE, steps), w_ple) * gate_p

    _rows_to_view(y_ref, _rmsnorm(h, g_final[...]), D_MODEL, steps, seqs)

    def write_state():
        _rows_to_view(na_ref, xa_ref[tm:tm + ha, :], D_A, K_A - 1, seqs)
        _rows_to_view(nb_ref, xb_ref[tm:tm + hb, :], D_B, K_B - 1, seqs)
        _rows_to_view(nf_ref, xf_ref[tm:tm + hf, :], D_FF, K_F - 1, seqs)

    if carry_history:
        pl.when(pl.program_id(0) == pl.num_programs(0) - 1)(write_state)
        xa_ref[0:ha, :] = xa_ref[tm:tm + ha, :]
        xb_ref[0:hb, :] = xb_ref[tm:tm + hb, :]
        xf_ref[0:hf, :] = xf_ref[tm:tm + hf, :]
    else:
        write_state()


def _resident(shape):
    return pl.BlockSpec(shape, lambda i: (0,) * len(shape),
                        pipeline_mode=pl.Buffered(1))


def _run_group(x, p, states, params, *, seqs, steps, name):
    n_seq, t_len, _ = x.shape
    carry_history = states is None
    if carry_history:
        assert seqs == n_seq and t_len % steps == 0
        grid = (t_len // steps,)
        tile = lambda width: pl.BlockSpec((seqs, steps * width), lambda i: (0, i))
        state_out = lambda width: pl.BlockSpec((seqs, width), lambda i: (0, 0))
        state_args, state_specs = [], []
    else:
        assert t_len == steps and n_seq % seqs == 0
        grid = (n_seq // seqs,)
        tile = lambda width: pl.BlockSpec((seqs, steps * width), lambda i: (i, 0))
        state_out = lambda width: pl.BlockSpec((seqs, width), lambda i: (i, 0))
        state_args = [s.reshape(n_seq, -1) for s in states]
        state_specs = [state_out(s.shape[1]) for s in state_args]
    assert (seqs * steps) % SUBLANES == 0 and seqs % SUBLANES == 0

    tm = seqs * steps
    out_shapes = (
        jax.ShapeDtypeStruct((n_seq, t_len * D_MODEL), jnp.float32),
        jax.ShapeDtypeStruct((n_seq, (K_A - 1) * D_A), jnp.float32),
        jax.ShapeDtypeStruct((n_seq, (K_B - 1) * D_B), jnp.float32),
        jax.ShapeDtypeStruct((n_seq, (K_F - 1) * D_FF), jnp.float32),
    )
    out_specs = (tile(D_MODEL), state_out((K_A - 1) * D_A),
                 state_out((K_B - 1) * D_B), state_out((K_F - 1) * D_FF))
    scratch = [
        pltpu.VMEM(((K_A - 1) * seqs + tm, D_A), jnp.float32),
        pltpu.VMEM(((K_B - 1) * seqs + tm, D_B), jnp.float32),
        pltpu.VMEM(((K_F - 1) * seqs + tm, D_FF), jnp.float32),
        pltpu.VMEM((tm, D_B), jnp.float32),
    ]
    y, na, nb, nf = pl.pallas_call(
        functools.partial(_layer_body, seqs=seqs, steps=steps,
                          carry_history=carry_history),
        grid=grid,
        in_specs=[tile(D_MODEL), tile(D_PLE)] + state_specs
                 + [_resident(w.shape) for w in params],
        out_specs=out_specs,
        out_shape=out_shapes,
        scratch_shapes=scratch,
        compiler_params=pltpu.CompilerParams(
            dimension_semantics=("arbitrary",),
            vmem_limit_bytes=VMEM_LIMIT_BYTES),
        name=name,
    )(x.reshape(n_seq, -1), p.reshape(n_seq, -1), *state_args, *params)
    return (y.reshape(n_seq, t_len, D_MODEL),
            na.reshape(1, n_seq, K_A - 1, D_A),
            nb.reshape(1, n_seq, K_B - 1, D_B),
            nf.reshape(1, n_seq, K_F - 1, D_FF))


def kernel(x_prompt, x_sample, p_prompt, p_sample, state_conv_a, state_conv_b, state_ffn_conv, g_mix, w_in, conv_a_w, conv_b_w, conv_b_b, ln_b_g, ln_b_b, w_out, g_ffn, w_up, conv_f_w, conv_f_b, w_down, g_ple, w_ple, w_ple_gate, g_final):
    assert g_mix.shape[0] == 1, "single-layer trunk"
    bf16 = lambda w: w[0].astype(jnp.bfloat16)
    row = lambda v: v.reshape(1, -1)
    params = (row(g_mix[0]), bf16(w_in), conv_a_w[0], conv_b_w[0], row(conv_b_b[0]),
              row(ln_b_g[0]), row(ln_b_b[0]), bf16(w_out), row(g_ffn[0]), bf16(w_up),
              conv_f_w[0], row(conv_f_b[0]), bf16(w_down), row(g_ple[0]), bf16(w_ple),
              bf16(w_ple_gate), row(g_final))
    yp, pa, pb, pf = _run_group(x_prompt, p_prompt[0], None, params,
                                seqs=x_prompt.shape[0], steps=32, name="prompt_layer")
    ys, sa, sb, sf = _run_group(x_sample, p_sample[0],
                                (state_conv_a[0], state_conv_b[0], state_ffn_conv[0]),
                                params, seqs=32, steps=x_sample.shape[1],
                                name="sample_layer")
    return (yp, ys, pa, pb, pf, sa, sb, sf)
```

```python
import functools

import jax
import jax.numpy as jnp
from jax import lax
from jax.experimental import pallas as pl
from jax.experimental.pallas import tpu as pltpu

D_MODEL = 1024
D_A = 512
D_B = 512
D_FF = 2816
D_PLE = 256
K_A = 3
K_B = 31
K_F = 3
EPS = 1e-6
D_IN = 3 * D_A + 2 * D_B

LANES = 128
SUBLANES = 8
MXU_COLS = 256
VMEM_LIMIT_BYTES = 60 * 1024 * 1024
CONV_GROUP = 8
CONV_TAPS_PER_PASS = 16
FFN_CHUNK = 4 * MXU_COLS


def _rmsnorm(x, g):
    return x * lax.rsqrt(jnp.mean(x * x, axis=-1, keepdims=True) + EPS) * g


def _layernorm(x, g, b):
    mu = jnp.mean(x, axis=-1, keepdims=True)
    xc = x - mu
    var = jnp.mean(xc * xc, axis=-1, keepdims=True)
    return xc * lax.rsqrt(var + EPS) * g + b


def _dot(a, w):
    return jnp.dot(a.astype(jnp.bfloat16), w, preferred_element_type=jnp.float32)


def _rows_from_view(ref, width, n_steps):
    return jnp.concatenate(
        [ref[:, t * width:(t + 1) * width] for t in range(n_steps)], axis=0)


def _rows_to_view(ref, rows, width, n_steps, seqs):
    for t in range(n_steps):
        ref[:, t * width:(t + 1) * width] = rows[t * seqs:(t + 1) * seqs, :]


def _conv_short(x_ref, w_ref, cols, seqs, tm):
    return (w_ref[0:1, cols] * x_ref[0:tm, cols]
            + w_ref[1:2, cols] * x_ref[seqs:seqs + tm, cols]
            + w_ref[2:3, cols] * x_ref[2 * seqs:2 * seqs + tm, cols])


def _conv_long(x_ref, w_ref, out_ref, j, seqs, steps, always):
    lanes = slice(j * LANES, (j + 1) * LANES)
    prev = {}
    for k0, k1 in ((0, CONV_TAPS_PER_PASS), (CONV_TAPS_PER_PASS, K_B)):
        taps = {k: jnp.broadcast_to(w_ref[k:k + 1, lanes], (SUBLANES, LANES))
                for k in range(k0, k1)}
        for s in range(0, seqs, SUBLANES):
            for g0 in range(0, steps, CONV_GROUP):
                group = range(g0, min(g0 + CONV_GROUP, steps))
                out_rows = {t: slice(t * seqs + s, t * seqs + s + SUBLANES) for t in group}
                acc = {t: (out_ref[out_rows[t], lanes] if k0 else None) for t in group}
                for tp in range(group[0] + k0, group[-1] + k1):
                    row = tp * seqs + s
                    x = x_ref[row:row + SUBLANES, lanes]
                    for t in group:
                        if k0 <= tp - t < k1:
                            term = taps[tp - t] * x
                            if tp - t == k0 and (t - g0) in prev:
                                term = jnp.where(always, term, prev[t - g0])
                            acc[t] = term if acc[t] is None else acc[t] + term
                for t in group:
                    out_ref[out_rows[t], lanes] = acc[t]
                    prev[t - g0] = acc[t]


def _layer_body(*refs, seqs, steps, carry_history):
    if carry_history:
        x_ref, p_ref = refs[:2]
        rest = refs[2:]
    else:
        x_ref, p_ref, sa_ref, sb_ref, sf_ref = refs[:5]
        rest = refs[5:]
    (g_mix, w_in, conv_a_w, conv_b_w, conv_b_b, ln_b_g, ln_b_b, w_out,
     g_ffn, w_up, conv_f_w, conv_f_b, w_down, g_ple, w_ple, w_ple_gate, g_final,
     y_ref, na_ref, nb_ref, nf_ref,
     xa_ref, xb_ref, xf_ref, yb_ref, hmid_ref) = rest

    tm = seqs * steps
    ha, hb, hf = (K_A - 1) * seqs, (K_B - 1) * seqs, (K_F - 1) * seqs
    step = pl.program_id(0)
    last = pl.num_programs(0) - 1

    @pl.when(step == 0)
    def _():
        hmid_ref[...] = jnp.zeros((tm, D_MODEL), jnp.float32)
        if carry_history:
            xa_ref[0:ha, :] = jnp.zeros((ha, D_A), jnp.float32)
            xb_ref[0:hb, :] = jnp.zeros((hb, D_B), jnp.float32)
            xf_ref[0:hf, :] = jnp.zeros((hf, D_FF), jnp.float32)

    if not carry_history:
        xa_ref[0:ha, :] = _rows_from_view(sa_ref, D_A, K_A - 1)
        xb_ref[0:hb, :] = _rows_from_view(sb_ref, D_B, K_B - 1)
        xf_ref[0:hf, :] = _rows_from_view(sf_ref, D_FF, K_F - 1)

    always = step <= last

    h_prev = hmid_ref[...]
    u_ffn = _rmsnorm(h_prev, g_ffn[...]).astype(jnp.bfloat16)

    h = _rows_from_view(x_ref, D_MODEL, steps)
    u_mix = _rmsnorm(h, g_mix[...]).astype(jnp.bfloat16)
    z_b = _dot(u_mix, w_in[:, 3 * D_A:D_IN])
    xb_ref[hb:hb + tm, :] = z_b[:, 0:D_B] * jax.nn.sigmoid(z_b[:, D_B:2 * D_B])

    ffn_chunks = [(c, min(c + FFN_CHUNK, D_FF)) for c in range(0, D_FF, FFN_CHUNK)]
    ffn_acc = None

    def ffn_chunk(c0, c1, acc):
        cols = slice(c0, c1)
        xf_ref[hf:hf + tm, cols] = _dot(u_ffn, w_up[:, c0:c1])
        val = _dot(u_ffn, w_up[:, D_FF + c0:D_FF + c1])
        gate = _conv_short(xf_ref, conv_f_w, cols, seqs, tm) + conv_f_b[:, cols]
        part = _dot(jax.nn.silu(gate) * val, w_down[c0:c1, :])
        return part if acc is None else acc + part

    ffn_acc = ffn_chunk(*ffn_chunks[0], ffn_acc)

    z_a = _dot(u_mix, w_in[:, 0:3 * D_A])
    xa_ref[ha:ha + tm, :] = z_a[:, D_A:2 * D_A] * z_a[:, 2 * D_A:3 * D_A]
    ya = z_a[:, 0:D_A] * _conv_short(xa_ref, conv_a_w, slice(0, D_A), seqs, tm)

    n_lane_tiles = D_B // LANES
    rest_chunks = ffn_chunks[1:]
    for j in range(n_lane_tiles):
        _conv_long(xb_ref, conv_b_w, yb_ref, j, seqs, steps, always)
        if j < len(rest_chunks):
            ffn_acc = ffn_chunk(*rest_chunks[j], ffn_acc)
    for c0, c1 in rest_chunks[n_lane_tiles:]:
        ffn_acc = ffn_chunk(c0, c1, ffn_acc)

    h_prev = h_prev + ffn_acc
    gate_p = jax.nn.sigmoid(_dot(_rmsnorm(h_prev, g_ple[...]), w_ple_gate[...]))
    h_prev = h_prev + _dot(_rows_from_view(p_ref, D_PLE, steps), w_ple[...]) * gate_p
    _rows_to_view(y_ref, _rmsnorm(h_prev, g_final[...]), D_MODEL, steps, seqs)

    yb = jax.nn.silu(_layernorm(yb_ref[...] + conv_b_b[...], ln_b_g[...], ln_b_b[...]))
    hmid_ref[...] = h + _dot(jnp.concatenate([ya, yb], axis=-1), w_out[...])

    def write_mixer_state():
        _rows_to_view(na_ref, xa_ref[tm:tm + ha, :], D_A, K_A - 1, seqs)
        _rows_to_view(nb_ref, xb_ref[tm:tm + hb, :], D_B, K_B - 1, seqs)

    def write_ffn_state():
        _rows_to_view(nf_ref, xf_ref[tm:tm + hf, :], D_FF, K_F - 1, seqs)

    if carry_history:
        pl.when(step == last - 1)(write_mixer_state)
        pl.when(step == last)(write_ffn_state)
        xa_ref[0:ha, :] = xa_ref[tm:tm + ha, :]
        xb_ref[0:hb, :] = xb_ref[tm:tm + hb, :]
        xf_ref[0:hf, :] = xf_ref[tm:tm + hf, :]
    else:
        write_mixer_state()
        write_ffn_state()


def _resident(shape):
    return pl.BlockSpec(shape, lambda i: (0,) * len(shape),
                        pipeline_mode=pl.Buffered(1))


def _run_group(x, p, states, params, *, seqs, steps, name):
    n_seq, t_len, _ = x.shape
    carry_history = states is None
    if carry_history:
        assert seqs == n_seq and t_len % steps == 0
        n_tiles = t_len // steps
        at = lambda tile: (0, tile)
    else:
        assert t_len == steps and n_seq % seqs == 0
        n_tiles = n_seq // seqs
        at = lambda tile: (tile, 0)
    assert seqs % SUBLANES == 0
    mixer_tile = lambda i: at(jnp.minimum(i, n_tiles - 1))
    ffn_tile = lambda i: at(jnp.maximum(i - 1, 0))

    if carry_history:
        state_args, state_specs = [], []
        mixer_state_out = ffn_state_out = lambda i: (0, 0)
    else:
        state_args = [s.reshape(n_seq, -1) for s in states]
        state_specs = [pl.BlockSpec((seqs, state_args[0].shape[1]), mixer_tile),
                       pl.BlockSpec((seqs, state_args[1].shape[1]), mixer_tile),
                       pl.BlockSpec((seqs, state_args[2].shape[1]), ffn_tile)]
        mixer_state_out, ffn_state_out = mixer_tile, ffn_tile

    tm = seqs * steps
    out_shapes = (
        jax.ShapeDtypeStruct((n_seq, t_len * D_MODEL), jnp.float32),
        jax.ShapeDtypeStruct((n_seq, (K_A - 1) * D_A), jnp.float32),
        jax.ShapeDtypeStruct((n_seq, (K_B - 1) * D_B), jnp.float32),
        jax.ShapeDtypeStruct((n_seq, (K_F - 1) * D_FF), jnp.float32),
    )
    out_specs = (pl.BlockSpec((seqs, steps * D_MODEL), ffn_tile),
                 pl.BlockSpec((seqs, (K_A - 1) * D_A), mixer_state_out),
                 pl.BlockSpec((seqs, (K_B - 1) * D_B), mixer_state_out),
                 pl.BlockSpec((seqs, (K_F - 1) * D_FF), ffn_state_out))
    scratch = [
        pltpu.VMEM(((K_A - 1) * seqs + tm, D_A), jnp.float32),
        pltpu.VMEM(((K_B - 1) * seqs + tm, D_B), jnp.float32),
        pltpu.VMEM(((K_F - 1) * seqs + tm, D_FF), jnp.float32),
        pltpu.VMEM((tm, D_B), jnp.float32),
        pltpu.VMEM((tm, D_MODEL), jnp.float32),
    ]
    y, na, nb, nf = pl.pallas_call(
        functools.partial(_layer_body, seqs=seqs, steps=steps,
                          carry_history=carry_history),
        grid=(n_tiles + 1,),
        in_specs=[pl.BlockSpec((seqs, steps * D_MODEL), mixer_tile),
                  pl.BlockSpec((seqs, steps * D_PLE), ffn_tile)] + state_specs
                 + [_resident(w.shape) for w in params],
        out_specs=out_specs,
        out_shape=out_shapes,
        scratch_shapes=scratch,
        compiler_params=pltpu.CompilerParams(
            dimension_semantics=("arbitrary",),
            vmem_limit_bytes=VMEM_LIMIT_BYTES),
        name=name,
    )(x.reshape(n_seq, -1), p.reshape(n_seq, -1), *state_args, *params)
    return (y.reshape(n_seq, t_len, D_MODEL),
            na.reshape(1, n_seq, K_A - 1, D_A),
            nb.reshape(1, n_seq, K_B - 1, D_B),
            nf.reshape(1, n_seq, K_F - 1, D_FF))


def kernel(x_prompt, x_sample, p_prompt, p_sample, state_conv_a, state_conv_b, state_ffn_conv, g_mix, w_in, conv_a_w, conv_b_w, conv_b_b, ln_b_g, ln_b_b, w_out, g_ffn, w_up, conv_f_w, conv_f_b, w_down, g_ple, w_ple, w_ple_gate, g_final):
    assert g_mix.shape[0] == 1, "single-layer trunk"
    bf16 = lambda w: w[0].astype(jnp.bfloat16)
    row = lambda v: v.reshape(1, -1)
    params = (row(g_mix[0]), bf16(w_in), conv_a_w[0], conv_b_w[0], row(conv_b_b[0]),
              row(ln_b_g[0]), row(ln_b_b[0]), bf16(w_out), row(g_ffn[0]), bf16(w_up),
              conv_f_w[0], row(conv_f_b[0]), bf16(w_down), row(g_ple[0]), bf16(w_ple),
              bf16(w_ple_gate), row(g_final))
    yp, pa, pb, pf = _run_group(x_prompt, p_prompt[0], None, params,
                                seqs=x_prompt.shape[0], steps=32, name="prompt_layer")
    ys, sa, sb, sf = _run_group(x_sample, p_sample[0],
                                (state_conv_a[0], state_conv_b[0], state_ffn_conv[0]),
                                params, seqs=32, steps=x_sample.shape[1],
                                name="sample_layer")
    return (yp, ys, pa, pb, pf, sa, sb, sf)
```

```python
import functools

import jax
import jax.numpy as jnp
from jax import lax
from jax.experimental import pallas as pl
from jax.experimental.pallas import tpu as pltpu

D_MODEL = 1024
D_A = 512
D_B = 512
D_FF = 2816
D_PLE = 256
K_A = 3
K_B = 31
K_F = 3
EPS = 1e-6
D_IN = 3 * D_A + 2 * D_B

LANES = 128
SUBLANES = 8
MXU_COLS = 256
VMEM_LIMIT_BYTES = 60 * 1024 * 1024
CONV_GROUP = 8
CONV_TAPS_PER_PASS = 16
FFN_CHUNK = 4 * MXU_COLS
PROMPT_STEPS = 32
PROMPT_PITCH = PROMPT_STEPS + SUBLANES
SAMPLE_SEQS = 32


def _rmsnorm(x, g):
    return x * lax.rsqrt(jnp.mean(x * x, axis=-1, keepdims=True) + EPS) * g


def _layernorm(x, g, b):
    mu = jnp.mean(x, axis=-1, keepdims=True)
    xc = x - mu
    var = jnp.mean(xc * xc, axis=-1, keepdims=True)
    return xc * lax.rsqrt(var + EPS) * g + b


def _dot(a, w):
    return jnp.dot(a.astype(jnp.bfloat16), w, preferred_element_type=jnp.float32)


def _rows_from_view(ref, width, n_steps):
    return jnp.concatenate(
        [ref[:, t * width:(t + 1) * width] for t in range(n_steps)], axis=0)


def _rows_to_view(ref, rows, width, n_steps, seqs):
    for t in range(n_steps):
        ref[:, t * width:(t + 1) * width] = rows[t * seqs:(t + 1) * seqs, :]


def _step_rows(slab, t, seqs):
    return slab[pl.ds(t, seqs, stride=PROMPT_PITCH), :]


def _rows_from_slabs(buf, slot, n_lane_tiles, n_steps, seqs):
    slabs = [buf.at[slot, j].reshape(seqs * PROMPT_PITCH, LANES) for j in range(n_lane_tiles)]
    return jnp.concatenate(
        [jnp.concatenate([_step_rows(slab, t, seqs) for slab in slabs], axis=1)
         for t in range(n_steps)], axis=0)


def _rows_to_slabs(buf, slot, rows, n_lane_tiles, n_steps, seqs):
    for j in range(n_lane_tiles):
        slab = buf.at[slot, j].reshape(seqs * PROMPT_PITCH, LANES)
        for t in range(n_steps):
            slab[pl.ds(t, seqs, stride=PROMPT_PITCH), :] = (
                rows[t * seqs:(t + 1) * seqs, j * LANES:(j + 1) * LANES])


def _conv_short(x_ref, w_ref, cols, seqs, tm):
    return (w_ref[0:1, cols] * x_ref[0:tm, cols]
            + w_ref[1:2, cols] * x_ref[seqs:seqs + tm, cols]
            + w_ref[2:3, cols] * x_ref[2 * seqs:2 * seqs + tm, cols])


def _conv_long(x_ref, w_ref, out_ref, j, seqs, steps, always):
    lanes = slice(j * LANES, (j + 1) * LANES)
    prev = {}
    for k0, k1 in ((0, CONV_TAPS_PER_PASS), (CONV_TAPS_PER_PASS, K_B)):
        taps = {k: jnp.broadcast_to(w_ref[k:k + 1, lanes], (SUBLANES, LANES))
                for k in range(k0, k1)}
        for s in range(0, seqs, SUBLANES):
            for g0 in range(0, steps, CONV_GROUP):
                group = range(g0, min(g0 + CONV_GROUP, steps))
                out_rows = {t: slice(t * seqs + s, t * seqs + s + SUBLANES) for t in group}
                acc = {t: (out_ref[out_rows[t], lanes] if k0 else None) for t in group}
                for tp in range(group[0] + k0, group[-1] + k1):
                    row = tp * seqs + s
                    x = x_ref[row:row + SUBLANES, lanes]
                    for t in group:
                        if k0 <= tp - t < k1:
                            term = taps[tp - t] * x
                            if tp - t == k0 and (t - g0) in prev:
                                term = jnp.where(always, term, prev[t - g0])
                            acc[t] = term if acc[t] is None else acc[t] + term
                for t in group:
                    out_ref[out_rows[t], lanes] = acc[t]
                    prev[t - g0] = acc[t]


def _layer_halves(h, get_p, put_y, params, scratch, *, seqs, steps, always):
    (g_mix, w_in, conv_a_w, conv_b_w, conv_b_b, ln_b_g, ln_b_b, w_out,
     g_ffn, w_up, conv_f_w, conv_f_b, w_down, g_ple, w_ple, w_ple_gate, g_final) = params
    xa_ref, xb_ref, xf_ref, yb_ref, hmid_ref = scratch
    tm = seqs * steps
    ha, hb, hf = (K_A - 1) * seqs, (K_B - 1) * seqs, (K_F - 1) * seqs

    h_prev = hmid_ref[...]
    u_ffn = _rmsnorm(h_prev, g_ffn[...]).astype(jnp.bfloat16)

    u_mix = _rmsnorm(h, g_mix[...]).astype(jnp.bfloat16)
    z_b = _dot(u_mix, w_in[:, 3 * D_A:D_IN])
    xb_ref[hb:hb + tm, :] = z_b[:, 0:D_B] * jax.nn.sigmoid(z_b[:, D_B:2 * D_B])

    ffn_chunks = [(c, min(c + FFN_CHUNK, D_FF)) for c in range(0, D_FF, FFN_CHUNK)]
    ffn_acc = None

    def ffn_chunk(c0, c1, acc):
        cols = slice(c0, c1)
        xf_ref[hf:hf + tm, cols] = _dot(u_ffn, w_up[:, c0:c1])
        val = _dot(u_ffn, w_up[:, D_FF + c0:D_FF + c1])
        gate = _conv_short(xf_ref, conv_f_w, cols, seqs, tm) + conv_f_b[:, cols]
        part = _dot(jax.nn.silu(gate) * val, w_down[c0:c1, :])
        return part if acc is None else acc + part

    ffn_acc = ffn_chunk(*ffn_chunks[0], ffn_acc)

    z_a = _dot(u_mix, w_in[:, 0:3 * D_A])
    xa_ref[ha:ha + tm, :] = z_a[:, D_A:2 * D_A] * z_a[:, 2 * D_A:3 * D_A]
    ya = z_a[:, 0:D_A] * _conv_short(xa_ref, conv_a_w, slice(0, D_A), seqs, tm)

    n_lane_tiles = D_B // LANES
    rest_chunks = ffn_chunks[1:]
    for j in range(n_lane_tiles):
        _conv_long(xb_ref, conv_b_w, yb_ref, j, seqs, steps, always)
        if j < len(rest_chunks):
            ffn_acc = ffn_chunk(*rest_chunks[j], ffn_acc)
    for c0, c1 in rest_chunks[n_lane_tiles:]:
        ffn_acc = ffn_chunk(c0, c1, ffn_acc)

    h_prev = h_prev + ffn_acc
    gate_p = jax.nn.sigmoid(_dot(_rmsnorm(h_prev, g_ple[...]), w_ple_gate[...]))
    h_prev = h_prev + _dot(get_p(), w_ple[...]) * gate_p
    put_y(_rmsnorm(h_prev, g_final[...]))

    yb = jax.nn.silu(_layernorm(yb_ref[...] + conv_b_b[...], ln_b_g[...], ln_b_b[...]))
    hmid_ref[...] = h + _dot(jnp.concatenate([ya, yb], axis=-1), w_out[...])


def _slab_copy(hbm, buf, sem, tile, slot, j, to_hbm):
    t0 = pl.multiple_of(tile * PROMPT_STEPS, PROMPT_STEPS)
    hbm_slab = hbm.at[:, pl.ds(t0, PROMPT_STEPS), pl.ds(j * LANES, LANES)]
    vmem_slab = buf.at[slot, j, :, pl.ds(0, PROMPT_STEPS), :]
    src, dst = (vmem_slab, hbm_slab) if to_hbm else (hbm_slab, vmem_slab)
    return pltpu.make_async_copy(src, dst, sem.at[slot, j])


def _prompt_body(x_hbm, p_hbm, *rest, seqs):
    params, rest = rest[:17], rest[17:]
    (y_hbm, na_ref, nb_ref, nf_ref,
     xa_ref, xb_ref, xf_ref, yb_ref, hmid_ref,
     xin, pin, yout, x_sem, p_sem, y_sem) = rest
    steps = PROMPT_STEPS
    tm = seqs * steps
    ha, hb, hf = (K_A - 1) * seqs, (K_B - 1) * seqs, (K_F - 1) * seqs
    x_tiles, p_tiles = D_MODEL // LANES, D_PLE // LANES
    step = pl.program_id(0)
    last = pl.num_programs(0) - 1
    mix_slot = lax.rem(step, 2)
    ffn_slot = 1 - mix_slot

    def x_copy(tile, slot, j):
        return _slab_copy(x_hbm, xin, x_sem, tile, slot, j, to_hbm=False)

    def p_copy(tile, slot, j):
        return _slab_copy(p_hbm, pin, p_sem, tile, slot, j, to_hbm=False)

    def y_copy(tile, slot, j):
        return _slab_copy(y_hbm, yout, y_sem, tile, slot, j, to_hbm=True)

    @pl.when(step == 0)
    def _():
        for j in range(x_tiles):
            x_copy(0, 0, j).start()
        hmid_ref[...] = jnp.zeros((tm, D_MODEL), jnp.float32)
        pin[1] = jnp.zeros(pin.shape[1:], jnp.float32)
        xa_ref[0:ha, :] = jnp.zeros((ha, D_A), jnp.float32)
        xb_ref[0:hb, :] = jnp.zeros((hb, D_B), jnp.float32)
        xf_ref[0:hf, :] = jnp.zeros((hf, D_FF), jnp.float32)

    @pl.when(step + 1 < last)
    def _():
        for j in range(x_tiles):
            x_copy(step + 1, ffn_slot, j).start()

    @pl.when(step < last)
    def _():
        for j in range(p_tiles):
            p_copy(step, mix_slot, j).start()
        for j in range(x_tiles):
            x_copy(step, mix_slot, j).wait()

    @pl.when(step >= 1)
    def _():
        for j in range(p_tiles):
            p_copy(step - 1, ffn_slot, j).wait()

    @pl.when(step >= 3)
    def _():
        for j in range(x_tiles):
            y_copy(step - 3, ffn_slot, j).wait()

    always = step <= last
    _layer_halves(
        _rows_from_slabs(xin, mix_slot, x_tiles, steps, seqs),
        lambda: _rows_from_slabs(pin, ffn_slot, p_tiles, steps, seqs),
        lambda y: _rows_to_slabs(yout, ffn_slot, y, x_tiles, steps, seqs),
        params, (xa_ref, xb_ref, xf_ref, yb_ref, hmid_ref),
        seqs=seqs, steps=steps, always=always)

    @pl.when(step >= 1)
    def _():
        for j in range(x_tiles):
            y_copy(step - 1, ffn_slot, j).start()

    @pl.when(step == last - 1)
    def _():
        _rows_to_view(na_ref, xa_ref[tm:tm + ha, :], D_A, K_A - 1, seqs)
        _rows_to_view(nb_ref, xb_ref[tm:tm + hb, :], D_B, K_B - 1, seqs)

    @pl.when(step == last)
    def _():
        _rows_to_view(nf_ref, xf_ref[tm:tm + hf, :], D_FF, K_F - 1, seqs)
        for j in range(x_tiles):
            y_copy(step - 2, mix_slot, j).wait()
            y_copy(step - 1, ffn_slot, j).wait()

    xa_ref[0:ha, :] = xa_ref[tm:tm + ha, :]
    xb_ref[0:hb, :] = xb_ref[tm:tm + hb, :]
    xf_ref[0:hf, :] = xf_ref[tm:tm + hf, :]


def _sample_body(x_ref, p_ref, sa_ref, sb_ref, sf_ref, *rest, seqs, steps):
    params, rest = rest[:17], rest[17:]
    (y_ref, na_ref, nb_ref, nf_ref,
     xa_ref, xb_ref, xf_ref, yb_ref, hmid_ref) = rest
    tm = seqs * steps
    ha, hb, hf = (K_A - 1) * seqs, (K_B - 1) * seqs, (K_F - 1) * seqs
    step = pl.program_id(0)
    last = pl.num_programs(0) - 1

    @pl.when(step == 0)
    def _():
        hmid_ref[...] = jnp.zeros((tm, D_MODEL), jnp.float32)

    xa_ref[0:ha, :] = _rows_from_view(sa_ref, D_A, K_A - 1)
    xb_ref[0:hb, :] = _rows_from_view(sb_ref, D_B, K_B - 1)
    xf_ref[0:hf, :] = _rows_from_view(sf_ref, D_FF, K_F - 1)

    always = step <= last
    _layer_halves(
        _rows_from_view(x_ref, D_MODEL, steps),
        lambda: _rows_from_view(p_ref, D_PLE, steps),
        lambda y: _rows_to_view(y_ref, y, D_MODEL, steps, seqs),
        params, (xa_ref, xb_ref, xf_ref, yb_ref, hmid_ref),
        seqs=seqs, steps=steps, always=always)

    _rows_to_view(na_ref, xa_ref[tm:tm + ha, :], D_A, K_A - 1, seqs)
    _rows_to_view(nb_ref, xb_ref[tm:tm + hb, :], D_B, K_B - 1, seqs)
    _rows_to_view(nf_ref, xf_ref[tm:tm + hf, :], D_FF, K_F - 1, seqs)


def _resident(shape):
    return pl.BlockSpec(shape, lambda i: (0,) * len(shape),
                        pipeline_mode=pl.Buffered(1))


def _layer_scratch(seqs, tm):
    return [
        pltpu.VMEM(((K_A - 1) * seqs + tm, D_A), jnp.float32),
        pltpu.VMEM(((K_B - 1) * seqs + tm, D_B), jnp.float32),
        pltpu.VMEM(((K_F - 1) * seqs + tm, D_FF), jnp.float32),
        pltpu.VMEM((tm, D_B), jnp.float32),
        pltpu.VMEM((tm, D_MODEL), jnp.float32),
    ]


def _state_shapes(n_seq):
    return (jax.ShapeDtypeStruct((n_seq, (K_A - 1) * D_A), jnp.float32),
            jax.ShapeDtypeStruct((n_seq, (K_B - 1) * D_B), jnp.float32),
            jax.ShapeDtypeStruct((n_seq, (K_F - 1) * D_FF), jnp.float32))


def _unflatten_states(na, nb, nf):
    n_seq = na.shape[0]
    return (na.reshape(1, n_seq, K_A - 1, D_A), nb.reshape(1, n_seq, K_B - 1, D_B),
            nf.reshape(1, n_seq, K_F - 1, D_FF))


def _run_prompt(x, p, params):
    n_seq, t_len, _ = x.shape
    assert n_seq == SUBLANES and t_len % PROMPT_STEPS == 0
    n_tiles = t_len // PROMPT_STEPS
    assert n_tiles >= 3, "the output ring waits on the copy issued three steps earlier"
    tm = n_seq * PROMPT_STEPS
    slabs = lambda width: pltpu.VMEM((2, width // LANES, n_seq, PROMPT_PITCH, LANES), jnp.float32)
    any_space = pl.BlockSpec(memory_space=pl.ANY)
    state_block = lambda s: pl.BlockSpec(s.shape, lambda i: (0, 0))
    y, na, nb, nf = pl.pallas_call(
        functools.partial(_prompt_body, seqs=n_seq),
        grid=(n_tiles + 1,),
        in_specs=[any_space, any_space] + [_resident(w.shape) for w in params],
        out_specs=(any_space,) + tuple(state_block(s) for s in _state_shapes(n_seq)),
        out_shape=(jax.ShapeDtypeStruct(x.shape, jnp.float32),) + _state_shapes(n_seq),
        scratch_shapes=_layer_scratch(n_seq, tm) + [
            slabs(D_MODEL), slabs(D_PLE), slabs(D_MODEL),
            pltpu.SemaphoreType.DMA((2, D_MODEL // LANES)),
            pltpu.SemaphoreType.DMA((2, D_PLE // LANES)),
            pltpu.SemaphoreType.DMA((2, D_MODEL // LANES))],
        compiler_params=pltpu.CompilerParams(
            dimension_semantics=("arbitrary",),
            vmem_limit_bytes=VMEM_LIMIT_BYTES),
        name="prompt_layer",
    )(x, p, *params)
    return (y,) + _unflatten_states(na, nb, nf)


def _run_sample(x, p, states, params):
    n_seq, steps, _ = x.shape
    seqs = SAMPLE_SEQS
    assert n_seq % seqs == 0 and seqs % SUBLANES == 0
    n_tiles = n_seq // seqs
    tm = seqs * steps
    mixer_tile = lambda i: (jnp.minimum(i, n_tiles - 1), 0)
    ffn_tile = lambda i: (jnp.maximum(i - 1, 0), 0)
    state_args = [s.reshape(n_seq, -1) for s in states]
    state_tiles = (mixer_tile, mixer_tile, ffn_tile)
    y, na, nb, nf = pl.pallas_call(
        functools.partial(_sample_body, seqs=seqs, steps=steps),
        grid=(n_tiles + 1,),
        in_specs=[pl.BlockSpec((seqs, steps * D_MODEL), mixer_tile),
                  pl.BlockSpec((seqs, steps * D_PLE), ffn_tile)]
                 + [pl.BlockSpec((seqs, s.shape[1]), tile)
                    for s, tile in zip(state_args, state_tiles)]
                 + [_resident(w.shape) for w in params],
        out_specs=(pl.BlockSpec((seqs, steps * D_MODEL), ffn_tile),)
                  + tuple(pl.BlockSpec((seqs, s.shape[1]), tile)
                          for s, tile in zip(_state_shapes(n_seq), state_tiles)),
        out_shape=(jax.ShapeDtypeStruct((n_seq, steps * D_MODEL), jnp.float32),)
                  + _state_shapes(n_seq),
        scratch_shapes=_layer_scratch(seqs, tm),
        compiler_params=pltpu.CompilerParams(
            dimension_semantics=("arbitrary",),
            vmem_limit_bytes=VMEM_LIMIT_BYTES),
        name="sample_layer",
    )(x.reshape(n_seq, -1), p.reshape(n_seq, -1), *state_args, *params)
    return (y.reshape(n_seq, steps, D_MODEL),) + _unflatten_states(na, nb, nf)


def kernel(x_prompt, x_sample, p_prompt, p_sample, state_conv_a, state_conv_b, state_ffn_conv, g_mix, w_in, conv_a_w, conv_b_w, conv_b_b, ln_b_g, ln_b_b, w_out, g_ffn, w_up, conv_f_w, conv_f_b, w_down, g_ple, w_ple, w_ple_gate, g_final):
    assert g_mix.shape[0] == 1, "single-layer trunk"
    bf16 = lambda w: w[0].astype(jnp.bfloat16)
    row = lambda v: v.reshape(1, -1)
    params = (row(g_mix[0]), bf16(w_in), conv_a_w[0], conv_b_w[0], row(conv_b_b[0]),
              row(ln_b_g[0]), row(ln_b_b[0]), bf16(w_out), row(g_ffn[0]), bf16(w_up),
              conv_f_w[0], row(conv_f_b[0]), bf16(w_down), row(g_ple[0]), bf16(w_ple),
              bf16(w_ple_gate), row(g_final))
    yp, pa, pb, pf = _run_prompt(x_prompt, p_prompt[0], params)
    ys, sa, sb, sf = _run_sample(x_sample, p_sample[0],
                                 (state_conv_a[0], state_conv_b[0], state_ffn_conv[0]), params)
    return (yp, ys, pa, pb, pf, sa, sb, sf)
```

```python
import functools

import jax
import jax.numpy as jnp
from jax import lax
from jax.experimental import pallas as pl
from jax.experimental.pallas import tpu as pltpu

D_MODEL = 1024
D_A = 512
D_B = 512
D_FF = 2816
D_PLE = 256
K_A = 3
K_B = 31
K_F = 3
EPS = 1e-6
D_IN = 3 * D_A + 2 * D_B

LANES = 128
SUBLANES = 8
MXU_COLS = 256
VMEM_LIMIT_BYTES = 60 * 1024 * 1024
CONV_GROUP = 8
FFN_CHUNK = 4 * MXU_COLS
HIST_SHORT = SUBLANES
HIST_LONG = -(-(K_B - 1) // SUBLANES) * SUBLANES
PROMPT_ROWS = 256
SAMPLE_SEQS = 16


def _rmsnorm(x, g):
    return x * lax.rsqrt(jnp.mean(x * x, axis=-1, keepdims=True) + EPS) * g


def _layernorm(x, g, b):
    mu = jnp.mean(x, axis=-1, keepdims=True)
    xc = x - mu
    var = jnp.mean(xc * xc, axis=-1, keepdims=True)
    return xc * lax.rsqrt(var + EPS) * g + b


def _dot(a, w):
    return jnp.dot(a.astype(jnp.bfloat16), w, preferred_element_type=jnp.float32)


def _store_segments(buf, val, lane_tile0, hist, n_seg, seg_rows):
    pitch = hist + seg_rows
    for jj in range(val.shape[1] // LANES):
        for s in range(n_seg):
            buf[lane_tile0 + jj, s * pitch + hist:s * pitch + hist + seg_rows, :] = (
                val[s * seg_rows:(s + 1) * seg_rows, jj * LANES:(jj + 1) * LANES])


def _conv_short(buf, w_ref, lane_tile0, n_lane_tiles, n_seg, seg_rows):
    pitch = HIST_SHORT + seg_rows
    cols = []
    for jt in range(lane_tile0, lane_tile0 + n_lane_tiles):
        lanes = slice(jt * LANES, (jt + 1) * LANES)
        pieces = []
        for s in range(n_seg):
            first = s * pitch + HIST_SHORT - (K_A - 1)
            pieces.append(sum(w_ref[k:k + 1, lanes] * buf[jt, first + k:first + k + seg_rows, :]
                              for k in range(K_A)))
        cols.append(jnp.concatenate(pieces, axis=0) if n_seg > 1 else pieces[0])
    return jnp.concatenate(cols, axis=1)


def _conv_long(buf, w_ref, out_ref, j, n_seg, seg_rows, always):
    lanes = slice(j * LANES, (j + 1) * LANES)
    pitch = HIST_LONG + seg_rows
    first = HIST_LONG - (K_B - 1)
    n_tiles = seg_rows // SUBLANES
    prev = {}
    for p in range(SUBLANES):
        ks = range(p, K_B, SUBLANES)
        taps = {k: jnp.broadcast_to(w_ref[k:k + 1, lanes], (SUBLANES, LANES)) for k in ks}
        for s in range(n_seg):
            for m0 in range(0, n_tiles, CONV_GROUP):
                tiles = range(m0, min(m0 + CONV_GROUP, n_tiles))
                rows = {m: slice(s * seg_rows + m * SUBLANES, s * seg_rows + (m + 1) * SUBLANES)
                        for m in tiles}
                acc = {m: (out_ref[rows[m], lanes] if p else None) for m in tiles}
                windows = {}
                for m in tiles:
                    for k in ks:
                        start = s * pitch + first + m * SUBLANES + k
                        if start not in windows:
                            windows[start] = buf[j, start:start + SUBLANES, :]
                        term = taps[k] * windows[start]
                        if k == p and (m - m0) in prev:
                            term = jnp.where(always, term, prev[m - m0])
                        acc[m] = term if acc[m] is None else acc[m] + term
                for m in tiles:
                    out_ref[rows[m], lanes] = acc[m]
                    prev[m - m0] = acc[m]


def _layer_halves(h, get_p, put_y, params, scratch, *, n_seg, seg_rows, always):
    (g_mix, w_in, conv_a_w, conv_b_w, conv_b_b, ln_b_g, ln_b_b, w_out,
     g_ffn, w_up, conv_f_w, conv_f_b, w_down, g_ple, w_ple, w_ple_gate, g_final) = params
    xa_buf, xb_buf, xf_buf, yb_ref, hmid_ref = scratch

    h_prev = hmid_ref[...]
    u_ffn = _rmsnorm(h_prev, g_ffn[...]).astype(jnp.bfloat16)

    u_mix = _rmsnorm(h, g_mix[...]).astype(jnp.bfloat16)
    z_b = _dot(u_mix, w_in[:, 3 * D_A:D_IN])
    _store_segments(xb_buf, z_b[:, 0:D_B] * jax.nn.sigmoid(z_b[:, D_B:2 * D_B]),
                    0, HIST_LONG, n_seg, seg_rows)

    ffn_chunks = [(c, min(c + FFN_CHUNK, D_FF)) for c in range(0, D_FF, FFN_CHUNK)]
    ffn_acc = None

    def ffn_chunk(c0, c1, acc):
        tile0, n_tiles = c0 // LANES, (c1 - c0) // LANES
        _store_segments(xf_buf, _dot(u_ffn, w_up[:, c0:c1]), tile0, HIST_SHORT, n_seg, seg_rows)
        val = _dot(u_ffn, w_up[:, D_FF + c0:D_FF + c1])
        gate = (_conv_short(xf_buf, conv_f_w, tile0, n_tiles, n_seg, seg_rows)
                + conv_f_b[:, c0:c1])
        part = _dot(jax.nn.silu(gate) * val, w_down[c0:c1, :])
        return part if acc is None else acc + part

    ffn_acc = ffn_chunk(*ffn_chunks[0], ffn_acc)

    z_a = _dot(u_mix, w_in[:, 0:3 * D_A])
    _store_segments(xa_buf, z_a[:, D_A:2 * D_A] * z_a[:, 2 * D_A:3 * D_A],
                    0, HIST_SHORT, n_seg, seg_rows)
    ya = z_a[:, 0:D_A] * _conv_short(xa_buf, conv_a_w, 0, D_A // LANES, n_seg, seg_rows)

    n_lane_tiles = D_B // LANES
    rest_chunks = ffn_chunks[1:]
    for j in range(n_lane_tiles):
        _conv_long(xb_buf, conv_b_w, yb_ref, j, n_seg, seg_rows, always)
        if j < len(rest_chunks):
            ffn_acc = ffn_chunk(*rest_chunks[j], ffn_acc)
    for c0, c1 in rest_chunks[n_lane_tiles:]:
        ffn_acc = ffn_chunk(c0, c1, ffn_acc)

    h_prev = h_prev + ffn_acc
    gate_p = jax.nn.sigmoid(_dot(_rmsnorm(h_prev, g_ple[...]), w_ple_gate[...]))
    h_prev = h_prev + _dot(get_p(), w_ple[...]) * gate_p
    put_y(_rmsnorm(h_prev, g_final[...]))

    yb = jax.nn.silu(_layernorm(yb_ref[...] + conv_b_b[...], ln_b_g[...], ln_b_b[...]))
    hmid_ref[...] = h + _dot(jnp.concatenate([ya, yb], axis=-1), w_out[...])


def _history_rows(buf, hist, k, seg, seg_rows, after):
    start = seg * (hist + seg_rows) + hist - (k - 1) + (seg_rows if after else 0)
    return slice(start, start + k - 1)


def _prompt_body(x_ref, p_ref, *rest, tiles_per_seq):
    params, rest = rest[:17], rest[17:]
    (y_ref, na_ref, nb_ref, nf_ref,
     xa_buf, xb_buf, xf_buf, yb_ref, hmid_ref) = rest
    seg_rows = PROMPT_ROWS
    step = pl.program_id(0)
    last = pl.num_programs(0) - 1
    mix_pos = lax.rem(jnp.minimum(step, last - 1), tiles_per_seq)
    ffn_pos = lax.rem(jnp.maximum(step - 1, 0), tiles_per_seq)

    @pl.when(step == 0)
    def _():
        hmid_ref[...] = jnp.zeros(hmid_ref.shape, jnp.float32)

    @pl.when(mix_pos == 0)
    def _():
        xa_buf[:, 0:HIST_SHORT, :] = jnp.zeros((D_A // LANES, HIST_SHORT, LANES), jnp.float32)
        xb_buf[:, 0:HIST_LONG, :] = jnp.zeros((D_B // LANES, HIST_LONG, LANES), jnp.float32)

    @pl.when(ffn_pos == 0)
    def _():
        xf_buf[:, 0:HIST_SHORT, :] = jnp.zeros((D_FF // LANES, HIST_SHORT, LANES), jnp.float32)

    def put_y(y):
        y_ref[0] = y

    always = step <= last
    _layer_halves(x_ref[0], lambda: p_ref[0], put_y, params,
                  (xa_buf, xb_buf, xf_buf, yb_ref, hmid_ref),
                  n_seg=1, seg_rows=seg_rows, always=always)

    new_a = _history_rows(xa_buf, HIST_SHORT, K_A, 0, seg_rows, after=True)
    new_b = _history_rows(xb_buf, HIST_LONG, K_B, 0, seg_rows, after=True)
    new_f = _history_rows(xf_buf, HIST_SHORT, K_F, 0, seg_rows, after=True)

    @pl.when((mix_pos == tiles_per_seq - 1) & (step < last))
    def _():
        for j in range(D_A // LANES):
            na_ref[0, :, j * LANES:(j + 1) * LANES] = xa_buf[j, new_a, :]
        for j in range(D_B // LANES):
            nb_ref[0, :, j * LANES:(j + 1) * LANES] = xb_buf[j, new_b, :]

    @pl.when((ffn_pos == tiles_per_seq - 1) & (step >= 1))
    def _():
        for j in range(D_FF // LANES):
            nf_ref[0, :, j * LANES:(j + 1) * LANES] = xf_buf[j, new_f, :]

    for buf, hist, k, new in ((xa_buf, HIST_SHORT, K_A, new_a), (xb_buf, HIST_LONG, K_B, new_b),
                              (xf_buf, HIST_SHORT, K_F, new_f)):
        old = _history_rows(buf, hist, k, 0, seg_rows, after=False)
        for j in range(buf.shape[0]):
            buf[j, old, :] = buf[j, new, :]


def _sample_body(x_ref, p_ref, sa_ref, sb_ref, sf_ref, *rest, seqs, steps):
    params, rest = rest[:17], rest[17:]
    (y_ref, na_ref, nb_ref, nf_ref,
     xa_buf, xb_buf, xf_buf, yb_ref, hmid_ref) = rest
    step = pl.program_id(0)
    last = pl.num_programs(0) - 1

    @pl.when(step == 0)
    def _():
        hmid_ref[...] = jnp.zeros(hmid_ref.shape, jnp.float32)

    def history(buf, hist, k, seg, after):
        return _history_rows(buf, hist, k, seg, steps, after)

    for s in range(seqs):
        for j in range(D_A // LANES):
            xa_buf[j, history(xa_buf, HIST_SHORT, K_A, s, False), :] = (
                sa_ref[s, :, j * LANES:(j + 1) * LANES])
        for j in range(D_B // LANES):
            xb_buf[j, history(xb_buf, HIST_LONG, K_B, s, False), :] = (
                sb_ref[s, :, j * LANES:(j + 1) * LANES])
        for j in range(D_FF // LANES):
            xf_buf[j, history(xf_buf, HIST_SHORT, K_F, s, False), :] = (
                sf_ref[s, :, j * LANES:(j + 1) * LANES])

    def put_y(y):
        for s in range(seqs):
            y_ref[s] = y[s * steps:(s + 1) * steps, :]

    always = step <= last
    _layer_halves(jnp.concatenate([x_ref[s] for s in range(seqs)], axis=0),
                  lambda: jnp.concatenate([p_ref[s] for s in range(seqs)], axis=0),
                  put_y, params, (xa_buf, xb_buf, xf_buf, yb_ref, hmid_ref),
                  n_seg=seqs, seg_rows=steps, always=always)

    for s in range(seqs):
        for j in range(D_A // LANES):
            na_ref[s, :, j * LANES:(j + 1) * LANES] = (
                xa_buf[j, history(xa_buf, HIST_SHORT, K_A, s, True), :])
        for j in range(D_B // LANES):
            nb_ref[s, :, j * LANES:(j + 1) * LANES] = (
                xb_buf[j, history(xb_buf, HIST_LONG, K_B, s, True), :])
        for j in range(D_FF // LANES):
            nf_ref[s, :, j * LANES:(j + 1) * LANES] = (
                xf_buf[j, history(xf_buf, HIST_SHORT, K_F, s, True), :])


def _resident(shape):
    return pl.BlockSpec(shape, lambda i: (0,) * len(shape),
                        pipeline_mode=pl.Buffered(1))


def _layer_scratch(n_seg, seg_rows):
    tm = n_seg * seg_rows
    conv_buf = lambda width, hist: pltpu.VMEM(
        (width // LANES, n_seg * (hist + seg_rows), LANES), jnp.float32)
    return [
        conv_buf(D_A, HIST_SHORT),
        conv_buf(D_B, HIST_LONG),
        conv_buf(D_FF, HIST_SHORT),
        pltpu.VMEM((tm, D_B), jnp.float32),
        pltpu.VMEM((tm, D_MODEL), jnp.float32),
    ]


def _state_shapes(n_seq):
    return (jax.ShapeDtypeStruct((n_seq, K_A - 1, D_A), jnp.float32),
            jax.ShapeDtypeStruct((n_seq, K_B - 1, D_B), jnp.float32),
            jax.ShapeDtypeStruct((n_seq, K_F - 1, D_FF), jnp.float32))


def _run_prompt(x, p, params):
    n_seq, t_len, _ = x.shape
    assert t_len % PROMPT_ROWS == 0
    tiles_per_seq = t_len // PROMPT_ROWS
    n_tiles = n_seq * tiles_per_seq

    def tile_at(tile):
        return (tile // tiles_per_seq, lax.rem(tile, tiles_per_seq), 0)

    mixer_tile = lambda g: tile_at(jnp.minimum(g, n_tiles - 1))
    ffn_tile = lambda g: tile_at(jnp.maximum(g - 1, 0))
    seq_of = lambda tile_fn: (lambda g: (tile_fn(g)[0], 0, 0))
    rows_block = lambda width, tile_fn: pl.BlockSpec((1, PROMPT_ROWS, width), tile_fn)
    state_block = lambda s, tile_fn: pl.BlockSpec((1,) + s.shape[1:], seq_of(tile_fn))
    sa, sb, sf = _state_shapes(n_seq)
    return pl.pallas_call(
        functools.partial(_prompt_body, tiles_per_seq=tiles_per_seq),
        grid=(n_tiles + 1,),
        in_specs=[rows_block(D_MODEL, mixer_tile), rows_block(D_PLE, ffn_tile)]
                 + [_resident(w.shape) for w in params],
        out_specs=(rows_block(D_MODEL, ffn_tile), state_block(sa, mixer_tile),
                   state_block(sb, mixer_tile), state_block(sf, ffn_tile)),
        out_shape=(jax.ShapeDtypeStruct(x.shape, jnp.float32), sa, sb, sf),
        scratch_shapes=_layer_scratch(1, PROMPT_ROWS),
        compiler_params=pltpu.CompilerParams(
            dimension_semantics=("arbitrary",),
            vmem_limit_bytes=VMEM_LIMIT_BYTES),
        name="prompt_layer",
    )(x, p, *params)


def _run_sample(x, p, states, params):
    n_seq, steps, _ = x.shape
    seqs = SAMPLE_SEQS
    assert n_seq % seqs == 0 and steps % SUBLANES == 0
    n_tiles = n_seq // seqs
    mixer_tile = lambda g: (jnp.minimum(g, n_tiles - 1), 0, 0)
    ffn_tile = lambda g: (jnp.maximum(g - 1, 0), 0, 0)
    block = lambda a, tile_fn: pl.BlockSpec((seqs,) + a.shape[1:], tile_fn)
    state_tiles = (mixer_tile, mixer_tile, ffn_tile)
    return pl.pallas_call(
        functools.partial(_sample_body, seqs=seqs, steps=steps),
        grid=(n_tiles + 1,),
        in_specs=[block(x, mixer_tile), block(p, ffn_tile)]
                 + [block(s, t) for s, t in zip(states, state_tiles)]
                 + [_resident(w.shape) for w in params],
        out_specs=(block(x, ffn_tile),)
                  + tuple(block(s, t) for s, t in zip(_state_shapes(n_seq), state_tiles)),
        out_shape=(jax.ShapeDtypeStruct(x.shape, jnp.float32),) + _state_shapes(n_seq),
        scratch_shapes=_layer_scratch(seqs, steps),
        compiler_params=pltpu.CompilerParams(
            dimension_semantics=("arbitrary",),
            vmem_limit_bytes=VMEM_LIMIT_BYTES),
        name="sample_layer",
    )(x, p, *states, *params)


def kernel(x_prompt, x_sample, p_prompt, p_sample, state_conv_a, state_conv_b, state_ffn_conv, g_mix, w_in, conv_a_w, conv_b_w, conv_b_b, ln_b_g, ln_b_b, w_out, g_ffn, w_up, conv_f_w, conv_f_b, w_down, g_ple, w_ple, w_ple_gate, g_final):
    assert g_mix.shape[0] == 1, "single-layer trunk"
    bf16 = lambda w: w[0].astype(jnp.bfloat16)
    row = lambda v: v.reshape(1, -1)
    params = (row(g_mix[0]), bf16(w_in), conv_a_w[0], conv_b_w[0], row(conv_b_b[0]),
              row(ln_b_g[0]), row(ln_b_b[0]), bf16(w_out), row(g_ffn[0]), bf16(w_up),
              conv_f_w[0], row(conv_f_b[0]), bf16(w_down), row(g_ple[0]), bf16(w_ple),
              bf16(w_ple_gate), row(g_final))
    yp, pa, pb, pf = _run_prompt(x_prompt, p_prompt[0], params)
    ys, sa, sb, sf = _run_sample(x_sample, p_sample[0],
                                 (state_conv_a[0], state_conv_b[0], state_ffn_conv[0]), params)
    return (yp, ys, pa[None], pb[None], pf[None], sa[None], sb[None], sf[None])
```

```python
import functools

import jax
import jax.numpy as jnp
from jax import lax
from jax.experimental import pallas as pl
from jax.experimental.pallas import tpu as pltpu

D_MODEL = 1024
D_A = 512
D_B = 512
D_FF = 2816
D_PLE = 256
K_A = 3
K_B = 31
K_F = 3
EPS = 1e-6
D_IN = 3 * D_A + 2 * D_B

LANES = 128
SUBLANES = 8
MXU_COLS = 256
VMEM_LIMIT_BYTES = 60 * 1024 * 1024
CONV_GROUP = 8
FFN_CHUNK = 4 * MXU_COLS
HIST_SHORT = SUBLANES
HIST_LONG = -(-(K_B - 1) // SUBLANES) * SUBLANES
PROMPT_ROWS = 512
SAMPLE_SEQS = 16


def _rmsnorm(x, g):
    return x * lax.rsqrt(jnp.mean(x * x, axis=-1, keepdims=True) + EPS) * g


def _layernorm(x, g, b):
    mu = jnp.mean(x, axis=-1, keepdims=True)
    xc = x - mu
    var = jnp.mean(xc * xc, axis=-1, keepdims=True)
    return xc * lax.rsqrt(var + EPS) * g + b


def _dot(a, w):
    return jnp.dot(a.astype(jnp.bfloat16), w, preferred_element_type=jnp.float32)


def _store_segments(buf, val, lane_tile0, hist, n_seg, seg_rows):
    pitch = hist + seg_rows
    for jj in range(val.shape[1] // LANES):
        for s in range(n_seg):
            buf[lane_tile0 + jj, s * pitch + hist:s * pitch + hist + seg_rows, :] = (
                val[s * seg_rows:(s + 1) * seg_rows, jj * LANES:(jj + 1) * LANES])


def _conv_short(buf, w_ref, lane_tile0, n_lane_tiles, n_seg, seg_rows):
    pitch = HIST_SHORT + seg_rows
    cols = []
    for jt in range(lane_tile0, lane_tile0 + n_lane_tiles):
        lanes = slice(jt * LANES, (jt + 1) * LANES)
        pieces = []
        for s in range(n_seg):
            first = s * pitch + HIST_SHORT - (K_A - 1)
            pieces.append(sum(w_ref[k:k + 1, lanes] * buf[jt, first + k:first + k + seg_rows, :]
                              for k in range(K_A)))
        cols.append(jnp.concatenate(pieces, axis=0) if n_seg > 1 else pieces[0])
    return jnp.concatenate(cols, axis=1)


def _conv_long(buf, w_ref, out_ref, j, n_seg, seg_rows, always):
    lanes = slice(j * LANES, (j + 1) * LANES)
    pitch = HIST_LONG + seg_rows
    first = HIST_LONG - (K_B - 1)
    n_tiles = seg_rows // SUBLANES
    prev = {}
    for p in range(SUBLANES):
        ks = range(p, K_B, SUBLANES)
        taps = {k: jnp.broadcast_to(w_ref[k:k + 1, lanes], (SUBLANES, LANES)) for k in ks}
        for s in range(n_seg):
            for m0 in range(0, n_tiles, CONV_GROUP):
                tiles = range(m0, min(m0 + CONV_GROUP, n_tiles))
                rows = {m: slice(s * seg_rows + m * SUBLANES, s * seg_rows + (m + 1) * SUBLANES)
                        for m in tiles}
                acc = {m: (out_ref[rows[m], lanes] if p else None) for m in tiles}
                windows = {}
                for m in tiles:
                    for k in ks:
                        start = s * pitch + first + m * SUBLANES + k
                        if start not in windows:
                            windows[start] = buf[j, start:start + SUBLANES, :]
                        term = taps[k] * windows[start]
                        if k == p and (m - m0) in prev:
                            term = jnp.where(always, term, prev[m - m0])
                        acc[m] = term if acc[m] is None else acc[m] + term
                for m in tiles:
                    out_ref[rows[m], lanes] = acc[m]
                    prev[m - m0] = acc[m]


def _layer_halves(h, get_p, put_y, params, scratch, *, n_seg, seg_rows, always):
    (g_mix, w_in, conv_a_w, conv_b_w, conv_b_b, ln_b_g, ln_b_b, w_out,
     g_ffn, w_up, conv_f_w, conv_f_b, w_down, g_ple, w_ple, w_ple_gate, g_final) = params
    xa_buf, xb_buf, xf_buf, yb_ref, hmid_ref = scratch

    h_prev = hmid_ref[...]
    u_ffn = _rmsnorm(h_prev, g_ffn[...]).astype(jnp.bfloat16)

    u_mix = _rmsnorm(h, g_mix[...]).astype(jnp.bfloat16)
    z_b = _dot(u_mix, w_in[:, 3 * D_A:D_IN])
    _store_segments(xb_buf, z_b[:, 0:D_B] * jax.nn.sigmoid(z_b[:, D_B:2 * D_B]),
                    0, HIST_LONG, n_seg, seg_rows)

    ffn_chunks = [(c, min(c + FFN_CHUNK, D_FF)) for c in range(0, D_FF, FFN_CHUNK)]
    ffn_acc = None

    def ffn_chunk(c0, c1, acc):
        tile0, n_tiles = c0 // LANES, (c1 - c0) // LANES
        _store_segments(xf_buf, _dot(u_ffn, w_up[:, c0:c1]), tile0, HIST_SHORT, n_seg, seg_rows)
        val = _dot(u_ffn, w_up[:, D_FF + c0:D_FF + c1])
        gate = (_conv_short(xf_buf, conv_f_w, tile0, n_tiles, n_seg, seg_rows)
                + conv_f_b[:, c0:c1])
        part = _dot(jax.nn.silu(gate) * val, w_down[c0:c1, :])
        return part if acc is None else acc + part

    ffn_acc = ffn_chunk(*ffn_chunks[0], ffn_acc)

    z_a = _dot(u_mix, w_in[:, 0:3 * D_A])
    _store_segments(xa_buf, z_a[:, D_A:2 * D_A] * z_a[:, 2 * D_A:3 * D_A],
                    0, HIST_SHORT, n_seg, seg_rows)
    ya = z_a[:, 0:D_A] * _conv_short(xa_buf, conv_a_w, 0, D_A // LANES, n_seg, seg_rows)

    n_lane_tiles = D_B // LANES
    rest_chunks = ffn_chunks[1:]
    for j in range(n_lane_tiles):
        _conv_long(xb_buf, conv_b_w, yb_ref, j, n_seg, seg_rows, always)
        if j < len(rest_chunks):
            ffn_acc = ffn_chunk(*rest_chunks[j], ffn_acc)
    for c0, c1 in rest_chunks[n_lane_tiles:]:
        ffn_acc = ffn_chunk(c0, c1, ffn_acc)

    h_prev = h_prev + ffn_acc
    gate_p = jax.nn.sigmoid(_dot(_rmsnorm(h_prev, g_ple[...]), w_ple_gate[...]))
    h_prev = h_prev + _dot(get_p(), w_ple[...]) * gate_p
    put_y(_rmsnorm(h_prev, g_final[...]))

    yb = jax.nn.silu(_layernorm(yb_ref[...] + conv_b_b[...], ln_b_g[...], ln_b_b[...]))
    hmid_ref[...] = h + _dot(jnp.concatenate([ya, yb], axis=-1), w_out[...])


def _history_rows(buf, hist, k, seg, seg_rows, after):
    start = seg * (hist + seg_rows) + hist - (k - 1) + (seg_rows if after else 0)
    return slice(start, start + k - 1)


def _prompt_body(x_ref, p_ref, *rest, tiles_per_seq):
    params, rest = rest[:17], rest[17:]
    (y_ref, na_ref, nb_ref, nf_ref,
     xa_buf, xb_buf, xf_buf, yb_ref, hmid_ref) = rest
    seg_rows = PROMPT_ROWS
    step = pl.program_id(0)
    last = pl.num_programs(0) - 1
    mix_pos = lax.rem(jnp.minimum(step, last - 1), tiles_per_seq)
    ffn_pos = lax.rem(jnp.maximum(step - 1, 0), tiles_per_seq)

    @pl.when(step == 0)
    def _():
        hmid_ref[...] = jnp.zeros(hmid_ref.shape, jnp.float32)

    @pl.when(mix_pos == 0)
    def _():
        xa_buf[:, 0:HIST_SHORT, :] = jnp.zeros((D_A // LANES, HIST_SHORT, LANES), jnp.float32)
        xb_buf[:, 0:HIST_LONG, :] = jnp.zeros((D_B // LANES, HIST_LONG, LANES), jnp.float32)

    @pl.when(ffn_pos == 0)
    def _():
        xf_buf[:, 0:HIST_SHORT, :] = jnp.zeros((D_FF // LANES, HIST_SHORT, LANES), jnp.float32)

    def put_y(y):
        y_ref[0] = y

    always = step <= last
    _layer_halves(x_ref[0], lambda: p_ref[0], put_y, params,
                  (xa_buf, xb_buf, xf_buf, yb_ref, hmid_ref),
                  n_seg=1, seg_rows=seg_rows, always=always)

    new_a = _history_rows(xa_buf, HIST_SHORT, K_A, 0, seg_rows, after=True)
    new_b = _history_rows(xb_buf, HIST_LONG, K_B, 0, seg_rows, after=True)
    new_f = _history_rows(xf_buf, HIST_SHORT, K_F, 0, seg_rows, after=True)

    @pl.when((mix_pos == tiles_per_seq - 1) & (step < last))
    def _():
        for j in range(D_A // LANES):
            na_ref[0, :, j * LANES:(j + 1) * LANES] = xa_buf[j, new_a, :]
        for j in range(D_B // LANES):
            nb_ref[0, :, j * LANES:(j + 1) * LANES] = xb_buf[j, new_b, :]

    @pl.when((ffn_pos == tiles_per_seq - 1) & (step >= 1))
    def _():
        for j in range(D_FF // LANES):
            nf_ref[0, :, j * LANES:(j + 1) * LANES] = xf_buf[j, new_f, :]

    for buf, hist, k, new in ((xa_buf, HIST_SHORT, K_A, new_a), (xb_buf, HIST_LONG, K_B, new_b),
                              (xf_buf, HIST_SHORT, K_F, new_f)):
        old = _history_rows(buf, hist, k, 0, seg_rows, after=False)
        for j in range(buf.shape[0]):
            buf[j, old, :] = buf[j, new, :]


def _sample_body(x_ref, p_ref, sa_ref, sb_ref, sf_ref, *rest, seqs, steps):
    params, rest = rest[:17], rest[17:]
    (y_ref, na_ref, nb_ref, nf_ref,
     xa_buf, xb_buf, xf_buf, yb_ref, hmid_ref) = rest
    step = pl.program_id(0)
    last = pl.num_programs(0) - 1

    @pl.when(step == 0)
    def _():
        hmid_ref[...] = jnp.zeros(hmid_ref.shape, jnp.float32)

    def history(buf, hist, k, seg, after):
        return _history_rows(buf, hist, k, seg, steps, after)

    for s in range(seqs):
        for j in range(D_A // LANES):
            xa_buf[j, history(xa_buf, HIST_SHORT, K_A, s, False), :] = (
                sa_ref[s, :, j * LANES:(j + 1) * LANES])
        for j in range(D_B // LANES):
            xb_buf[j, history(xb_buf, HIST_LONG, K_B, s, False), :] = (
                sb_ref[s, :, j * LANES:(j + 1) * LANES])
        for j in range(D_FF // LANES):
            xf_buf[j, history(xf_buf, HIST_SHORT, K_F, s, False), :] = (
                sf_ref[s, :, j * LANES:(j + 1) * LANES])

    def put_y(y):
        for s in range(seqs):
            y_ref[s] = y[s * steps:(s + 1) * steps, :]

    always = step <= last
    _layer_halves(jnp.concatenate([x_ref[s] for s in range(seqs)], axis=0),
                  lambda: jnp.concatenate([p_ref[s] for s in range(seqs)], axis=0),
                  put_y, params, (xa_buf, xb_buf, xf_buf, yb_ref, hmid_ref),
                  n_seg=seqs, seg_rows=steps, always=always)

    for s in range(seqs):
        for j in range(D_A // LANES):
            na_ref[s, :, j * LANES:(j + 1) * LANES] = (
                xa_buf[j, history(xa_buf, HIST_SHORT, K_A, s, True), :])
        for j in range(D_B // LANES):
            nb_ref[s, :, j * LANES:(j + 1) * LANES] = (
                xb_buf[j, history(xb_buf, HIST_LONG, K_B, s, True), :])
        for j in range(D_FF // LANES):
            nf_ref[s, :, j * LANES:(j + 1) * LANES] = (
                xf_buf[j, history(xf_buf, HIST_SHORT, K_F, s, True), :])


def _resident(shape):
    block = shape if len(shape) == 2 else (None,) + tuple(shape[1:])
    return pl.BlockSpec(block, lambda i: (0,) * len(shape),
                        pipeline_mode=pl.Buffered(1))


def _layer_scratch(n_seg, seg_rows):
    tm = n_seg * seg_rows
    conv_buf = lambda width, hist: pltpu.VMEM(
        (width // LANES, n_seg * (hist + seg_rows), LANES), jnp.float32)
    return [
        conv_buf(D_A, HIST_SHORT),
        conv_buf(D_B, HIST_LONG),
        conv_buf(D_FF, HIST_SHORT),
        pltpu.VMEM((tm, D_B), jnp.float32),
        pltpu.VMEM((tm, D_MODEL), jnp.float32),
    ]


def _state_shapes(n_seq):
    return (jax.ShapeDtypeStruct((n_seq, K_A - 1, D_A), jnp.float32),
            jax.ShapeDtypeStruct((n_seq, K_B - 1, D_B), jnp.float32),
            jax.ShapeDtypeStruct((n_seq, K_F - 1, D_FF), jnp.float32))


def _run_prompt(x, p, params):
    n_seq, t_len, _ = x.shape
    assert t_len % PROMPT_ROWS == 0
    tiles_per_seq = t_len // PROMPT_ROWS
    n_tiles = n_seq * tiles_per_seq

    def tile_at(tile):
        return (tile // tiles_per_seq, lax.rem(tile, tiles_per_seq), 0)

    mixer_tile = lambda g: tile_at(jnp.minimum(g, n_tiles - 1))
    ffn_tile = lambda g: tile_at(jnp.maximum(g - 1, 0))
    seq_of = lambda tile_fn: (lambda g: (tile_fn(g)[0], 0, 0))
    rows_block = lambda width, tile_fn: pl.BlockSpec((1, PROMPT_ROWS, width), tile_fn)
    state_block = lambda s, tile_fn: pl.BlockSpec((1,) + s.shape[1:], seq_of(tile_fn))
    sa, sb, sf = _state_shapes(n_seq)
    return pl.pallas_call(
        functools.partial(_prompt_body, tiles_per_seq=tiles_per_seq),
        grid=(n_tiles + 1,),
        in_specs=[rows_block(D_MODEL, mixer_tile), rows_block(D_PLE, ffn_tile)]
                 + [_resident(w.shape) for w in params],
        out_specs=(rows_block(D_MODEL, ffn_tile), state_block(sa, mixer_tile),
                   state_block(sb, mixer_tile), state_block(sf, ffn_tile)),
        out_shape=(jax.ShapeDtypeStruct(x.shape, jnp.float32), sa, sb, sf),
        scratch_shapes=_layer_scratch(1, PROMPT_ROWS),
        compiler_params=pltpu.CompilerParams(
            dimension_semantics=("arbitrary",),
            vmem_limit_bytes=VMEM_LIMIT_BYTES),
        name="prompt_layer",
    )(x, p, *params)


def _run_sample(x, p, states, params):
    n_seq, steps, _ = x.shape
    seqs = SAMPLE_SEQS
    assert n_seq % seqs == 0 and steps % SUBLANES == 0
    n_tiles = n_seq // seqs
    mixer_tile = lambda g: (jnp.minimum(g, n_tiles - 1), 0, 0)
    ffn_tile = lambda g: (jnp.maximum(g - 1, 0), 0, 0)
    block = lambda a, tile_fn: pl.BlockSpec((seqs,) + a.shape[1:], tile_fn)
    state_tiles = (mixer_tile, mixer_tile, ffn_tile)
    return pl.pallas_call(
        functools.partial(_sample_body, seqs=seqs, steps=steps),
        grid=(n_tiles + 1,),
        in_specs=[block(x, mixer_tile), block(p, ffn_tile)]
                 + [block(s, t) for s, t in zip(states, state_tiles)]
                 + [_resident(w.shape) for w in params],
        out_specs=(block(x, ffn_tile),)
                  + tuple(block(s, t) for s, t in zip(_state_shapes(n_seq), state_tiles)),
        out_shape=(jax.ShapeDtypeStruct(x.shape, jnp.float32),) + _state_shapes(n_seq),
        scratch_shapes=_layer_scratch(seqs, steps),
        compiler_params=pltpu.CompilerParams(
            dimension_semantics=("arbitrary",),
            vmem_limit_bytes=VMEM_LIMIT_BYTES),
        name="sample_layer",
    )(x, p, *states, *params)


def kernel(x_prompt, x_sample, p_prompt, p_sample, state_conv_a, state_conv_b, state_ffn_conv, g_mix, w_in, conv_a_w, conv_b_w, conv_b_b, ln_b_g, ln_b_b, w_out, g_ffn, w_up, conv_f_w, conv_f_b, w_down, g_ple, w_ple, w_ple_gate, g_final):
    assert g_mix.shape[0] == 1, "single-layer trunk"
    bf16 = lambda w: w[0].astype(jnp.bfloat16)
    row = lambda v: v.reshape(1, -1)
    params = (row(g_mix[0]), bf16(w_in), conv_a_w, conv_b_w, row(conv_b_b[0]),
              row(ln_b_g[0]), row(ln_b_b[0]), bf16(w_out), row(g_ffn[0]), bf16(w_up),
              conv_f_w, row(conv_f_b[0]), bf16(w_down), row(g_ple[0]), bf16(w_ple),
              bf16(w_ple_gate), row(g_final))
    yp, pa, pb, pf = _run_prompt(x_prompt, p_prompt[0], params)
    ys, sa, sb, sf = _run_sample(x_sample, p_sample[0],
                                 (state_conv_a[0], state_conv_b[0], state_ffn_conv[0]), params)
    return (yp, ys, pa[None], pb[None], pf[None], sa[None], sb[None], sf[None])
```

```python
import functools

import jax
import jax.numpy as jnp
from jax import lax
from jax.experimental import pallas as pl
from jax.experimental.pallas import tpu as pltpu

D_MODEL = 1024
D_A = 512
D_B = 512
D_FF = 2816
D_PLE = 256
K_A = 3
K_B = 31
K_F = 3
EPS = 1e-6
D_IN = 3 * D_A + 2 * D_B

LANES = 128
SUBLANES = 8
MXU_COLS = 256
VMEM_LIMIT_BYTES = 60 * 1024 * 1024
CONV_GROUP = 8
FFN_CHUNK = 4 * MXU_COLS
HIST_SHORT = SUBLANES
HIST_LONG = -(-(K_B - 1) // SUBLANES) * SUBLANES
PROMPT_ROWS = 512
MATRIX_PARAMS = (1, 7, 9, 12, 14, 15)
STAGE_ROWS, STAGE_COLS, STAGE_SLOTS = 256, 1024, 4
SAMPLE_SEQS = 16


def _rmsnorm(x, g):
    return x * lax.rsqrt(jnp.mean(x * x, axis=-1, keepdims=True) + EPS) * g


def _layernorm(x, g, b):
    mu = jnp.mean(x, axis=-1, keepdims=True)
    xc = x - mu
    var = jnp.mean(xc * xc, axis=-1, keepdims=True)
    return xc * lax.rsqrt(var + EPS) * g + b


def _dot(a, w):
    return jnp.dot(a.astype(jnp.bfloat16), w, preferred_element_type=jnp.float32)


def _store_segments(buf, val, lane_tile0, hist, n_seg, seg_rows):
    pitch = hist + seg_rows
    for jj in range(val.shape[1] // LANES):
        for s in range(n_seg):
            buf[lane_tile0 + jj, s * pitch + hist:s * pitch + hist + seg_rows, :] = (
                val[s * seg_rows:(s + 1) * seg_rows, jj * LANES:(jj + 1) * LANES])


def _conv_short(buf, w_ref, lane_tile0, n_lane_tiles, n_seg, seg_rows):
    pitch = HIST_SHORT + seg_rows
    cols = []
    for jt in range(lane_tile0, lane_tile0 + n_lane_tiles):
        lanes = slice(jt * LANES, (jt + 1) * LANES)
        pieces = []
        for s in range(n_seg):
            first = s * pitch + HIST_SHORT - (K_A - 1)
            pieces.append(sum(w_ref[k:k + 1, lanes] * buf[jt, first + k:first + k + seg_rows, :]
                              for k in range(K_A)))
        cols.append(jnp.concatenate(pieces, axis=0) if n_seg > 1 else pieces[0])
    return jnp.concatenate(cols, axis=1)


def _conv_long(buf, w_ref, out_ref, j, n_seg, seg_rows, always):
    lanes = slice(j * LANES, (j + 1) * LANES)
    pitch = HIST_LONG + seg_rows
    first = HIST_LONG - (K_B - 1)
    n_tiles = seg_rows // SUBLANES
    prev = {}
    for p in range(SUBLANES):
        ks = range(p, K_B, SUBLANES)
        taps = {k: jnp.broadcast_to(w_ref[k:k + 1, lanes], (SUBLANES, LANES)) for k in ks}
        for s in range(n_seg):
            for m0 in range(0, n_tiles, CONV_GROUP):
                tiles = range(m0, min(m0 + CONV_GROUP, n_tiles))
                rows = {m: slice(s * seg_rows + m * SUBLANES, s * seg_rows + (m + 1) * SUBLANES)
                        for m in tiles}
                acc = {m: (out_ref[rows[m], lanes] if p else None) for m in tiles}
                windows = {}
                for m in tiles:
                    for k in ks:
                        start = s * pitch + first + m * SUBLANES + k
                        if start not in windows:
                            windows[start] = buf[j, start:start + SUBLANES, :]
                        term = taps[k] * windows[start]
                        if k == p and (m - m0) in prev:
                            term = jnp.where(always, term, prev[m - m0])
                        acc[m] = term if acc[m] is None else acc[m] + term
                for m in tiles:
                    out_ref[rows[m], lanes] = acc[m]
                    prev[m - m0] = acc[m]


def _layer_halves(h, get_p, put_y, params, scratch, *, n_seg, seg_rows, always):
    (g_mix, w_in, conv_a_w, conv_b_w, conv_b_b, ln_b_g, ln_b_b, w_out,
     g_ffn, w_up, conv_f_w, conv_f_b, w_down, g_ple, w_ple, w_ple_gate, g_final) = params
    xa_buf, xb_buf, xf_buf, yb_ref, hmid_ref = scratch

    h_prev = hmid_ref[...]
    u_ffn = _rmsnorm(h_prev, g_ffn[...]).astype(jnp.bfloat16)

    u_mix = _rmsnorm(h, g_mix[...]).astype(jnp.bfloat16)
    z_b = _dot(u_mix, w_in[:, 3 * D_A:D_IN])
    _store_segments(xb_buf, z_b[:, 0:D_B] * jax.nn.sigmoid(z_b[:, D_B:2 * D_B]),
                    0, HIST_LONG, n_seg, seg_rows)

    ffn_chunks = [(c, min(c + FFN_CHUNK, D_FF)) for c in range(0, D_FF, FFN_CHUNK)]
    ffn_acc = None

    def ffn_chunk(c0, c1, acc):
        tile0, n_tiles = c0 // LANES, (c1 - c0) // LANES
        _store_segments(xf_buf, _dot(u_ffn, w_up[:, c0:c1]), tile0, HIST_SHORT, n_seg, seg_rows)
        val = _dot(u_ffn, w_up[:, D_FF + c0:D_FF + c1])
        gate = (_conv_short(xf_buf, conv_f_w, tile0, n_tiles, n_seg, seg_rows)
                + conv_f_b[:, c0:c1])
        part = _dot(jax.nn.silu(gate) * val, w_down[c0:c1, :])
        return part if acc is None else acc + part

    ffn_acc = ffn_chunk(*ffn_chunks[0], ffn_acc)

    z_a = _dot(u_mix, w_in[:, 0:3 * D_A])
    _store_segments(xa_buf, z_a[:, D_A:2 * D_A] * z_a[:, 2 * D_A:3 * D_A],
                    0, HIST_SHORT, n_seg, seg_rows)
    ya = z_a[:, 0:D_A] * _conv_short(xa_buf, conv_a_w, 0, D_A // LANES, n_seg, seg_rows)

    n_lane_tiles = D_B // LANES
    rest_chunks = ffn_chunks[1:]
    for j in range(n_lane_tiles):
        _conv_long(xb_buf, conv_b_w, yb_ref, j, n_seg, seg_rows, always)
        if j < len(rest_chunks):
            ffn_acc = ffn_chunk(*rest_chunks[j], ffn_acc)
    for c0, c1 in rest_chunks[n_lane_tiles:]:
        ffn_acc = ffn_chunk(c0, c1, ffn_acc)

    h_prev = h_prev + ffn_acc
    gate_p = jax.nn.sigmoid(_dot(_rmsnorm(h_prev, g_ple[...]), w_ple_gate[...]))
    h_prev = h_prev + _dot(get_p(), w_ple[...]) * gate_p
    put_y(_rmsnorm(h_prev, g_final[...]))

    yb = jax.nn.silu(_layernorm(yb_ref[...] + conv_b_b[...], ln_b_g[...], ln_b_b[...]))
    hmid_ref[...] = h + _dot(jnp.concatenate([ya, yb], axis=-1), w_out[...])


def _history_rows(buf, hist, k, seg, seg_rows, after):
    start = seg * (hist + seg_rows) + hist - (k - 1) + (seg_rows if after else 0)
    return slice(start, start + k - 1)


def _stage_chunks(shapes):
    return [(n, r0, c0, min(STAGE_COLS, cols - c0))
            for n, (rows, cols) in enumerate(shapes)
            for r0 in range(0, rows, STAGE_ROWS)
            for c0 in range(0, cols, STAGE_COLS)]


def _convert_matrices(w_hbm, w_vmem, stage, sem):
    chunks = _stage_chunks([w.shape for w in w_vmem])

    def copy(i):
        n, r0, c0, cols = chunks[i]
        slot = i % STAGE_SLOTS
        return pltpu.make_async_copy(
            w_hbm[n].at[0, pl.ds(r0, STAGE_ROWS), pl.ds(c0, cols)],
            stage.at[slot, :, pl.ds(0, cols)], sem.at[slot])

    ahead = STAGE_SLOTS - 1
    for i in range(min(ahead, len(chunks))):
        copy(i).start()
    for i, (n, r0, c0, cols) in enumerate(chunks):
        if i + ahead < len(chunks):
            copy(i + ahead).start()
        copy(i).wait()
        w_vmem[n][r0:r0 + STAGE_ROWS, c0:c0 + cols] = (
            stage[i % STAGE_SLOTS, :, 0:cols].astype(jnp.bfloat16))


def _prompt_body(x_ref, p_ref, *rest, tiles_per_seq):
    params, rest = list(rest[:17]), rest[17:]
    n_mat = len(MATRIX_PARAMS)
    (y_ref, na_ref, nb_ref, nf_ref), rest = rest[:4], rest[4:]
    w_bf16_hbm, rest = rest[:n_mat], rest[n_mat:]
    (xa_buf, xb_buf, xf_buf, yb_ref, hmid_ref), rest = rest[:5], rest[5:]
    w_vmem, (stage, stage_sem, w_out_sem) = rest[:n_mat], rest[n_mat:]
    w_f32_hbm = [params[i] for i in MATRIX_PARAMS]
    for i, w in zip(MATRIX_PARAMS, w_vmem):
        params[i] = w
    seg_rows = PROMPT_ROWS
    step = pl.program_id(0)
    last = pl.num_programs(0) - 1
    mix_pos = lax.rem(jnp.minimum(step, last - 1), tiles_per_seq)
    ffn_pos = lax.rem(jnp.maximum(step - 1, 0), tiles_per_seq)

    def w_out_copy(n):
        return pltpu.make_async_copy(w_vmem[n], w_bf16_hbm[n], w_out_sem.at[n])

    @pl.when(step == 0)
    def _():
        _convert_matrices(w_f32_hbm, w_vmem, stage, stage_sem)
        for n in range(n_mat):
            w_out_copy(n).start()
        hmid_ref[...] = jnp.zeros(hmid_ref.shape, jnp.float32)

    @pl.when(step == last)
    def _():
        for n in range(n_mat):
            w_out_copy(n).wait()

    @pl.when(mix_pos == 0)
    def _():
        xa_buf[:, 0:HIST_SHORT, :] = jnp.zeros((D_A // LANES, HIST_SHORT, LANES), jnp.float32)
        xb_buf[:, 0:HIST_LONG, :] = jnp.zeros((D_B // LANES, HIST_LONG, LANES), jnp.float32)

    @pl.when(ffn_pos == 0)
    def _():
        xf_buf[:, 0:HIST_SHORT, :] = jnp.zeros((D_FF // LANES, HIST_SHORT, LANES), jnp.float32)

    def put_y(y):
        y_ref[0] = y

    always = step <= last
    _layer_halves(x_ref[0], lambda: p_ref[0], put_y, params,
                  (xa_buf, xb_buf, xf_buf, yb_ref, hmid_ref),
                  n_seg=1, seg_rows=seg_rows, always=always)

    new_a = _history_rows(xa_buf, HIST_SHORT, K_A, 0, seg_rows, after=True)
    new_b = _history_rows(xb_buf, HIST_LONG, K_B, 0, seg_rows, after=True)
    new_f = _history_rows(xf_buf, HIST_SHORT, K_F, 0, seg_rows, after=True)

    @pl.when((mix_pos == tiles_per_seq - 1) & (step < last))
    def _():
        for j in range(D_A // LANES):
            na_ref[0, :, j * LANES:(j + 1) * LANES] = xa_buf[j, new_a, :]
        for j in range(D_B // LANES):
            nb_ref[0, :, j * LANES:(j + 1) * LANES] = xb_buf[j, new_b, :]

    @pl.when((ffn_pos == tiles_per_seq - 1) & (step >= 1))
    def _():
        for j in range(D_FF // LANES):
            nf_ref[0, :, j * LANES:(j + 1) * LANES] = xf_buf[j, new_f, :]

    for buf, hist, k, new in ((xa_buf, HIST_SHORT, K_A, new_a), (xb_buf, HIST_LONG, K_B, new_b),
                              (xf_buf, HIST_SHORT, K_F, new_f)):
        old = _history_rows(buf, hist, k, 0, seg_rows, after=False)
        for j in range(buf.shape[0]):
            buf[j, old, :] = buf[j, new, :]


def _sample_body(x_ref, p_ref, sa_ref, sb_ref, sf_ref, *rest, seqs, steps):
    params, rest = rest[:17], rest[17:]
    (y_ref, na_ref, nb_ref, nf_ref,
     xa_buf, xb_buf, xf_buf, yb_ref, hmid_ref) = rest
    step = pl.program_id(0)
    last = pl.num_programs(0) - 1

    @pl.when(step == 0)
    def _():
        hmid_ref[...] = jnp.zeros(hmid_ref.shape, jnp.float32)

    def history(buf, hist, k, seg, after):
        return _history_rows(buf, hist, k, seg, steps, after)

    for s in range(seqs):
        for j in range(D_A // LANES):
            xa_buf[j, history(xa_buf, HIST_SHORT, K_A, s, False), :] = (
                sa_ref[s, :, j * LANES:(j + 1) * LANES])
        for j in range(D_B // LANES):
            xb_buf[j, history(xb_buf, HIST_LONG, K_B, s, False), :] = (
                sb_ref[s, :, j * LANES:(j + 1) * LANES])
        for j in range(D_FF // LANES):
            xf_buf[j, history(xf_buf, HIST_SHORT, K_F, s, False), :] = (
                sf_ref[s, :, j * LANES:(j + 1) * LANES])

    def put_y(y):
        for s in range(seqs):
            y_ref[s] = y[s * steps:(s + 1) * steps, :]

    always = step <= last
    _layer_halves(jnp.concatenate([x_ref[s] for s in range(seqs)], axis=0),
                  lambda: jnp.concatenate([p_ref[s] for s in range(seqs)], axis=0),
                  put_y, params, (xa_buf, xb_buf, xf_buf, yb_ref, hmid_ref),
                  n_seg=seqs, seg_rows=steps, always=always)

    for s in range(seqs):
        for j in range(D_A // LANES):
            na_ref[s, :, j * LANES:(j + 1) * LANES] = (
                xa_buf[j, history(xa_buf, HIST_SHORT, K_A, s, True), :])
        for j in range(D_B // LANES):
            nb_ref[s, :, j * LANES:(j + 1) * LANES] = (
                xb_buf[j, history(xb_buf, HIST_LONG, K_B, s, True), :])
        for j in range(D_FF // LANES):
            nf_ref[s, :, j * LANES:(j + 1) * LANES] = (
                xf_buf[j, history(xf_buf, HIST_SHORT, K_F, s, True), :])


def _resident(shape):
    block = shape if len(shape) == 2 else (None,) + tuple(shape[1:])
    return pl.BlockSpec(block, lambda i: (0,) * len(shape),
                        pipeline_mode=pl.Buffered(1))


def _layer_scratch(n_seg, seg_rows):
    tm = n_seg * seg_rows
    conv_buf = lambda width, hist: pltpu.VMEM(
        (width // LANES, n_seg * (hist + seg_rows), LANES), jnp.float32)
    return [
        conv_buf(D_A, HIST_SHORT),
        conv_buf(D_B, HIST_LONG),
        conv_buf(D_FF, HIST_SHORT),
        pltpu.VMEM((tm, D_B), jnp.float32),
        pltpu.VMEM((tm, D_MODEL), jnp.float32),
    ]


def _state_shapes(n_seq):
    return (jax.ShapeDtypeStruct((n_seq, K_A - 1, D_A), jnp.float32),
            jax.ShapeDtypeStruct((n_seq, K_B - 1, D_B), jnp.float32),
            jax.ShapeDtypeStruct((n_seq, K_F - 1, D_FF), jnp.float32))


def _run_prompt(x, p, params):
    n_seq, t_len, _ = x.shape
    assert t_len % PROMPT_ROWS == 0
    tiles_per_seq = t_len // PROMPT_ROWS
    n_tiles = n_seq * tiles_per_seq

    def tile_at(tile):
        return (tile // tiles_per_seq, lax.rem(tile, tiles_per_seq), 0)

    mixer_tile = lambda g: tile_at(jnp.minimum(g, n_tiles - 1))
    ffn_tile = lambda g: tile_at(jnp.maximum(g - 1, 0))
    seq_of = lambda tile_fn: (lambda g: (tile_fn(g)[0], 0, 0))
    rows_block = lambda width, tile_fn: pl.BlockSpec((1, PROMPT_ROWS, width), tile_fn)
    state_block = lambda s, tile_fn: pl.BlockSpec((1,) + s.shape[1:], seq_of(tile_fn))
    sa, sb, sf = _state_shapes(n_seq)
    any_space = pl.BlockSpec(memory_space=pl.ANY)
    matrices = [params[i].shape[1:] for i in MATRIX_PARAMS]
    assert all(k % STAGE_ROWS == 0 for k, _ in matrices)
    outs = pl.pallas_call(
        functools.partial(_prompt_body, tiles_per_seq=tiles_per_seq),
        grid=(n_tiles + 1,),
        in_specs=[rows_block(D_MODEL, mixer_tile), rows_block(D_PLE, ffn_tile)]
                 + [any_space if i in MATRIX_PARAMS else _resident(w.shape)
                    for i, w in enumerate(params)],
        out_specs=(rows_block(D_MODEL, ffn_tile), state_block(sa, mixer_tile),
                   state_block(sb, mixer_tile), state_block(sf, ffn_tile))
                  + (any_space,) * len(matrices),
        out_shape=(jax.ShapeDtypeStruct(x.shape, jnp.float32), sa, sb, sf)
                  + tuple(jax.ShapeDtypeStruct(m, jnp.bfloat16) for m in matrices),
        scratch_shapes=_layer_scratch(1, PROMPT_ROWS)
                       + [pltpu.VMEM(m, jnp.bfloat16) for m in matrices]
                       + [pltpu.VMEM((STAGE_SLOTS, STAGE_ROWS, STAGE_COLS), jnp.float32),
                          pltpu.SemaphoreType.DMA((STAGE_SLOTS,)),
                          pltpu.SemaphoreType.DMA((len(matrices),))],
        compiler_params=pltpu.CompilerParams(
            dimension_semantics=("arbitrary",),
            vmem_limit_bytes=VMEM_LIMIT_BYTES),
        name="prompt_layer",
    )(x, p, *params)
    return outs[:4], outs[4:]


def _run_sample(x, p, states, params):
    n_seq, steps, _ = x.shape
    seqs = SAMPLE_SEQS
    assert n_seq % seqs == 0 and steps % SUBLANES == 0
    n_tiles = n_seq // seqs
    mixer_tile = lambda g: (jnp.minimum(g, n_tiles - 1), 0, 0)
    ffn_tile = lambda g: (jnp.maximum(g - 1, 0), 0, 0)
    block = lambda a, tile_fn: pl.BlockSpec((seqs,) + a.shape[1:], tile_fn)
    state_tiles = (mixer_tile, mixer_tile, ffn_tile)
    return pl.pallas_call(
        functools.partial(_sample_body, seqs=seqs, steps=steps),
        grid=(n_tiles + 1,),
        in_specs=[block(x, mixer_tile), block(p, ffn_tile)]
                 + [block(s, t) for s, t in zip(states, state_tiles)]
                 + [_resident(w.shape) for w in params],
        out_specs=(block(x, ffn_tile),)
                  + tuple(block(s, t) for s, t in zip(_state_shapes(n_seq), state_tiles)),
        out_shape=(jax.ShapeDtypeStruct(x.shape, jnp.float32),) + _state_shapes(n_seq),
        scratch_shapes=_layer_scratch(seqs, steps),
        compiler_params=pltpu.CompilerParams(
            dimension_semantics=("arbitrary",),
            vmem_limit_bytes=VMEM_LIMIT_BYTES),
        name="sample_layer",
    )(x, p, *states, *params)


def kernel(x_prompt, x_sample, p_prompt, p_sample, state_conv_a, state_conv_b, state_ffn_conv, g_mix, w_in, conv_a_w, conv_b_w, conv_b_b, ln_b_g, ln_b_b, w_out, g_ffn, w_up, conv_f_w, conv_f_b, w_down, g_ple, w_ple, w_ple_gate, g_final):
    assert g_mix.shape[0] == 1, "single-layer trunk"
    row = lambda v: v.reshape(1, -1)
    params = [row(g_mix[0]), w_in, conv_a_w, conv_b_w, row(conv_b_b[0]),
              row(ln_b_g[0]), row(ln_b_b[0]), w_out, row(g_ffn[0]), w_up,
              conv_f_w, row(conv_f_b[0]), w_down, row(g_ple[0]), w_ple,
              w_ple_gate, row(g_final)]
    (yp, pa, pb, pf), matrices_bf16 = _run_prompt(x_prompt, p_prompt[0], params)
    for i, w in zip(MATRIX_PARAMS, matrices_bf16):
        params[i] = w
    ys, sa, sb, sf = _run_sample(x_sample, p_sample[0],
                                 (state_conv_a[0], state_conv_b[0], state_ffn_conv[0]), params)
    return (yp, ys, pa[None], pb[None], pf[None], sa[None], sb[None], sf[None])
```

```python
import functools

import jax
import jax.numpy as jnp
from jax import lax
from jax.experimental import pallas as pl
from jax.experimental.pallas import tpu as pltpu

D_MODEL = 1024
D_A = 512
D_B = 512
D_FF = 2816
D_PLE = 256
K_A = 3
K_B = 31
K_F = 3
EPS = 1e-6
D_IN = 3 * D_A + 2 * D_B

LANES = 128
SUBLANES = 8
MXU_COLS = 256
VMEM_LIMIT_BYTES = 60 * 1024 * 1024
CONV_GROUP = 8
FFN_CHUNK = 4 * MXU_COLS
HIST_SHORT = SUBLANES
HIST_LONG = -(-(K_B - 1) // SUBLANES) * SUBLANES
PROMPT_ROWS = 512
MATRIX_PARAMS = (1, 7, 9, 12, 14, 15)
STAGE_ROWS, STAGE_COLS, STAGE_SLOTS = 256, 1024, 4
SAMPLE_SEQS = 16


def _rmsnorm(x, g):
    return x * lax.rsqrt(jnp.mean(x * x, axis=-1, keepdims=True) + EPS) * g


def _layernorm(x, g, b):
    mu = jnp.mean(x, axis=-1, keepdims=True)
    xc = x - mu
    var = jnp.mean(xc * xc, axis=-1, keepdims=True)
    return xc * lax.rsqrt(var + EPS) * g + b


def _dot(a, w):
    return jnp.dot(a.astype(jnp.bfloat16), w, preferred_element_type=jnp.float32)


def _store_segments(buf, val, lane_tile0, hist, n_seg, seg_rows):
    pitch = hist + seg_rows
    for jj in range(val.shape[1] // LANES):
        for s in range(n_seg):
            buf[lane_tile0 + jj, s * pitch + hist:s * pitch + hist + seg_rows, :] = (
                val[s * seg_rows:(s + 1) * seg_rows, jj * LANES:(jj + 1) * LANES])


def _conv_short(buf, w_ref, lane_tile0, n_lane_tiles, n_seg, seg_rows):
    pitch = HIST_SHORT + seg_rows
    cols = []
    for jt in range(lane_tile0, lane_tile0 + n_lane_tiles):
        lanes = slice(jt * LANES, (jt + 1) * LANES)
        pieces = []
        for s in range(n_seg):
            first = s * pitch + HIST_SHORT - (K_A - 1)
            pieces.append(sum(w_ref[k:k + 1, lanes] * buf[jt, first + k:first + k + seg_rows, :]
                              for k in range(K_A)))
        cols.append(jnp.concatenate(pieces, axis=0) if n_seg > 1 else pieces[0])
    return jnp.concatenate(cols, axis=1)


def _conv_long(buf, w_ref, out_ref, j, n_seg, seg_rows, always):
    lanes = slice(j * LANES, (j + 1) * LANES)
    pitch = HIST_LONG + seg_rows
    first = HIST_LONG - (K_B - 1)
    n_tiles = seg_rows // SUBLANES
    prev = {}
    for p in range(SUBLANES):
        ks = range(p, K_B, SUBLANES)
        taps = {k: jnp.broadcast_to(w_ref[k:k + 1, lanes], (SUBLANES, LANES)) for k in ks}
        for s in range(n_seg):
            for m0 in range(0, n_tiles, CONV_GROUP):
                tiles = range(m0, min(m0 + CONV_GROUP, n_tiles))
                rows = {m: slice(s * seg_rows + m * SUBLANES, s * seg_rows + (m + 1) * SUBLANES)
                        for m in tiles}
                acc = {m: (out_ref[rows[m], lanes] if p else None) for m in tiles}
                windows = {}
                for m in tiles:
                    for k in ks:
                        start = s * pitch + first + m * SUBLANES + k
                        if start not in windows:
                            windows[start] = buf[j, start:start + SUBLANES, :]
                        term = taps[k] * windows[start]
                        if k == p and (m - m0) in prev:
                            term = jnp.where(always, term, prev[m - m0])
                        acc[m] = term if acc[m] is None else acc[m] + term
                for m in tiles:
                    out_ref[rows[m], lanes] = acc[m]
                    prev[m - m0] = acc[m]


def _layer_halves(h, get_p, put_y, params, scratch, *, n_seg, seg_rows, always):
    (g_mix, w_in, conv_a_w, conv_b_w, conv_b_b, ln_b_g, ln_b_b, w_out,
     g_ffn, w_up, conv_f_w, conv_f_b, w_down, g_ple, w_ple, w_ple_gate, g_final) = params
    xa_buf, xb_buf, xf_buf, yb_ref, hmid_ref = scratch

    h_prev = hmid_ref[...]
    u_ffn = _rmsnorm(h_prev, g_ffn[...]).astype(jnp.bfloat16)

    u_mix = _rmsnorm(h, g_mix[...]).astype(jnp.bfloat16)
    z_b = _dot(u_mix, w_in[:, 3 * D_A:D_IN])
    _store_segments(xb_buf, z_b[:, 0:D_B] * jax.nn.sigmoid(z_b[:, D_B:2 * D_B]),
                    0, HIST_LONG, n_seg, seg_rows)

    ffn_chunks = [(c, min(c + FFN_CHUNK, D_FF)) for c in range(0, D_FF, FFN_CHUNK)]
    ffn_acc = None

    def ffn_chunk(c0, c1, acc):
        tile0, n_tiles = c0 // LANES, (c1 - c0) // LANES
        _store_segments(xf_buf, _dot(u_ffn, w_up[:, c0:c1]), tile0, HIST_SHORT, n_seg, seg_rows)
        val = _dot(u_ffn, w_up[:, D_FF + c0:D_FF + c1])
        gate = (_conv_short(xf_buf, conv_f_w, tile0, n_tiles, n_seg, seg_rows)
                + conv_f_b[:, c0:c1])
        part = _dot(jax.nn.silu(gate) * val, w_down[c0:c1, :])
        return part if acc is None else acc + part

    ffn_acc = ffn_chunk(*ffn_chunks[0], ffn_acc)

    z_a = _dot(u_mix, w_in[:, 0:3 * D_A])
    _store_segments(xa_buf, z_a[:, D_A:2 * D_A] * z_a[:, 2 * D_A:3 * D_A],
                    0, HIST_SHORT, n_seg, seg_rows)
    ya = z_a[:, 0:D_A] * _conv_short(xa_buf, conv_a_w, 0, D_A // LANES, n_seg, seg_rows)

    n_lane_tiles = D_B // LANES
    rest_chunks = ffn_chunks[1:]
    for j in range(n_lane_tiles):
        _conv_long(xb_buf, conv_b_w, yb_ref, j, n_seg, seg_rows, always)
        if j < len(rest_chunks):
            ffn_acc = ffn_chunk(*rest_chunks[j], ffn_acc)
    for c0, c1 in rest_chunks[n_lane_tiles:]:
        ffn_acc = ffn_chunk(c0, c1, ffn_acc)

    h_prev = h_prev + ffn_acc
    gate_p = jax.nn.sigmoid(_dot(_rmsnorm(h_prev, g_ple[...]), w_ple_gate[...]))
    h_prev = h_prev + _dot(get_p(), w_ple[...]) * gate_p
    put_y(_rmsnorm(h_prev, g_final[...]))

    yb = jax.nn.silu(_layernorm(yb_ref[...] + conv_b_b[...], ln_b_g[...], ln_b_b[...]))
    hmid_ref[...] = h + _dot(jnp.concatenate([ya, yb], axis=-1), w_out[...])


def _history_rows(buf, hist, k, seg, seg_rows, after):
    start = seg * (hist + seg_rows) + hist - (k - 1) + (seg_rows if after else 0)
    return slice(start, start + k - 1)


def _stage_chunks(shapes):
    return [(n, r0, c0, min(STAGE_COLS, cols - c0))
            for n, (rows, cols) in enumerate(shapes)
            for r0 in range(0, rows, STAGE_ROWS)
            for c0 in range(0, cols, STAGE_COLS)]


def _convert_matrices(w_hbm, w_vmem, stage, sem):
    chunks = _stage_chunks([w.shape for w in w_vmem])

    def copy(i):
        n, r0, c0, cols = chunks[i]
        slot = i % STAGE_SLOTS
        return pltpu.make_async_copy(
            w_hbm[n].at[0, pl.ds(r0, STAGE_ROWS), pl.ds(c0, cols)],
            stage.at[slot, :, pl.ds(0, cols)], sem.at[slot])

    ahead = STAGE_SLOTS - 1
    for i in range(min(ahead, len(chunks))):
        copy(i).start()
    for i, (n, r0, c0, cols) in enumerate(chunks):
        if i + ahead < len(chunks):
            copy(i + ahead).start()
        copy(i).wait()
        w_vmem[n][r0:r0 + STAGE_ROWS, c0:c0 + cols] = (
            stage[i % STAGE_SLOTS, :, 0:cols].astype(jnp.bfloat16))


def _prompt_body(x_ref, p_ref, *rest, tiles_per_seq):
    params, rest = list(rest[:17]), rest[17:]
    n_mat = len(MATRIX_PARAMS)
    (y_ref, na_ref, nb_ref, nf_ref), rest = rest[:4], rest[4:]
    w_bf16_hbm, rest = rest[:n_mat], rest[n_mat:]
    (xa_buf, xb_buf, xf_buf, yb_ref, hmid_ref), rest = rest[:5], rest[5:]
    w_vmem, (stage, stage_sem, w_out_sem) = rest[:n_mat], rest[n_mat:]
    w_f32_hbm = [params[i] for i in MATRIX_PARAMS]
    for i, w in zip(MATRIX_PARAMS, w_vmem):
        params[i] = w
    seg_rows = PROMPT_ROWS
    step = pl.program_id(0)
    last = pl.num_programs(0) - 1
    mix_pos = lax.rem(jnp.minimum(step, last - 1), tiles_per_seq)
    ffn_pos = lax.rem(jnp.maximum(step - 1, 0), tiles_per_seq)

    def w_out_copy(n):
        return pltpu.make_async_copy(w_vmem[n], w_bf16_hbm[n], w_out_sem.at[n])

    @pl.when(step == 0)
    def _():
        _convert_matrices(w_f32_hbm, w_vmem, stage, stage_sem)
        for n in range(n_mat):
            w_out_copy(n).start()
        hmid_ref[...] = jnp.zeros(hmid_ref.shape, jnp.float32)

    @pl.when(step == last)
    def _():
        for n in range(n_mat):
            w_out_copy(n).wait()

    @pl.when(mix_pos == 0)
    def _():
        xa_buf[:, 0:HIST_SHORT, :] = jnp.zeros((D_A // LANES, HIST_SHORT, LANES), jnp.float32)
        xb_buf[:, 0:HIST_LONG, :] = jnp.zeros((D_B // LANES, HIST_LONG, LANES), jnp.float32)

    @pl.when(ffn_pos == 0)
    def _():
        xf_buf[:, 0:HIST_SHORT, :] = jnp.zeros((D_FF // LANES, HIST_SHORT, LANES), jnp.float32)

    def put_y(y):
        y_ref[0] = y

    always = step <= last
    _layer_halves(x_ref[0], lambda: p_ref[0], put_y, params,
                  (xa_buf, xb_buf, xf_buf, yb_ref, hmid_ref),
                  n_seg=1, seg_rows=seg_rows, always=always)

    new_a = _history_rows(xa_buf, HIST_SHORT, K_A, 0, seg_rows, after=True)
    new_b = _history_rows(xb_buf, HIST_LONG, K_B, 0, seg_rows, after=True)
    new_f = _history_rows(xf_buf, HIST_SHORT, K_F, 0, seg_rows, after=True)

    @pl.when((mix_pos == tiles_per_seq - 1) & (step < last))
    def _():
        for j in range(D_A // LANES):
            na_ref[0, :, j * LANES:(j + 1) * LANES] = xa_buf[j, new_a, :]
        for j in range(D_B // LANES):
            nb_ref[0, :, j * LANES:(j + 1) * LANES] = xb_buf[j, new_b, :]

    @pl.when((ffn_pos == tiles_per_seq - 1) & (step >= 1))
    def _():
        for j in range(D_FF // LANES):
            nf_ref[0, :, j * LANES:(j + 1) * LANES] = xf_buf[j, new_f, :]

    for buf, hist, k, new in ((xa_buf, HIST_SHORT, K_A, new_a), (xb_buf, HIST_LONG, K_B, new_b),
                              (xf_buf, HIST_SHORT, K_F, new_f)):
        old = _history_rows(buf, hist, k, 0, seg_rows, after=False)
        for j in range(buf.shape[0]):
            buf[j, old, :] = buf[j, new, :]


def _sample_body(x_ref, p_ref, sa_ref, sb_ref, sf_ref, *rest, seqs, steps):
    params, rest = rest[:17], rest[17:]
    (y_ref, na_ref, nb_ref, nf_ref,
     xa_buf, xb_buf, xf_buf, yb_ref, hmid_ref) = rest
    step = pl.program_id(0)
    last = pl.num_programs(0) - 1

    @pl.when(step == 0)
    def _():
        hmid_ref[...] = jnp.zeros(hmid_ref.shape, jnp.float32)

    def history(buf, hist, k, seg, after):
        return _history_rows(buf, hist, k, seg, steps, after)

    for s in range(seqs):
        for j in range(D_A // LANES):
            xa_buf[j, history(xa_buf, HIST_SHORT, K_A, s, False), :] = (
                sa_ref[s, :, j * LANES:(j + 1) * LANES])
        for j in range(D_FF // LANES):
            xf_buf[j, history(xf_buf, HIST_SHORT, K_F, s, False), :] = (
                sf_ref[s, :, j * LANES:(j + 1) * LANES])
    pitch_b = HIST_LONG + steps
    old_b = history(xb_buf, HIST_LONG, K_B, 0, False).start
    new_b = history(xb_buf, HIST_LONG, K_B, 0, True).start
    for k in range(K_B - 1):
        for j in range(D_B // LANES):
            xb_buf.at[j][pl.ds(old_b + k, seqs, stride=pitch_b), :] = (
                sb_ref[k, :, j * LANES:(j + 1) * LANES])

    def put_y(y):
        for s in range(seqs):
            y_ref[s] = y[s * steps:(s + 1) * steps, :]

    always = step <= last
    _layer_halves(jnp.concatenate([x_ref[s] for s in range(seqs)], axis=0),
                  lambda: jnp.concatenate([p_ref[s] for s in range(seqs)], axis=0),
                  put_y, params, (xa_buf, xb_buf, xf_buf, yb_ref, hmid_ref),
                  n_seg=seqs, seg_rows=steps, always=always)

    for s in range(seqs):
        for j in range(D_A // LANES):
            na_ref[s, :, j * LANES:(j + 1) * LANES] = (
                xa_buf[j, history(xa_buf, HIST_SHORT, K_A, s, True), :])
        for j in range(D_FF // LANES):
            nf_ref[s, :, j * LANES:(j + 1) * LANES] = (
                xf_buf[j, history(xf_buf, HIST_SHORT, K_F, s, True), :])
    for k in range(K_B - 1):
        for j in range(D_B // LANES):
            nb_ref[k, :, j * LANES:(j + 1) * LANES] = (
                xb_buf.at[j][pl.ds(new_b + k, seqs, stride=pitch_b), :])


def _resident(shape):
    block = shape if len(shape) == 2 else (None,) + tuple(shape[1:])
    return pl.BlockSpec(block, lambda i: (0,) * len(shape),
                        pipeline_mode=pl.Buffered(1))


def _layer_scratch(n_seg, seg_rows):
    tm = n_seg * seg_rows
    conv_buf = lambda width, hist: pltpu.VMEM(
        (width // LANES, n_seg * (hist + seg_rows), LANES), jnp.float32)
    return [
        conv_buf(D_A, HIST_SHORT),
        conv_buf(D_B, HIST_LONG),
        conv_buf(D_FF, HIST_SHORT),
        pltpu.VMEM((tm, D_B), jnp.float32),
        pltpu.VMEM((tm, D_MODEL), jnp.float32),
    ]


def _state_shapes(n_seq):
    return (jax.ShapeDtypeStruct((n_seq, K_A - 1, D_A), jnp.float32),
            jax.ShapeDtypeStruct((n_seq, K_B - 1, D_B), jnp.float32),
            jax.ShapeDtypeStruct((n_seq, K_F - 1, D_FF), jnp.float32))


def _run_prompt(x, p, params):
    n_seq, t_len, _ = x.shape
    assert t_len % PROMPT_ROWS == 0
    tiles_per_seq = t_len // PROMPT_ROWS
    n_tiles = n_seq * tiles_per_seq

    def tile_at(tile):
        return (tile // tiles_per_seq, lax.rem(tile, tiles_per_seq), 0)

    mixer_tile = lambda g: tile_at(jnp.minimum(g, n_tiles - 1))
    ffn_tile = lambda g: tile_at(jnp.maximum(g - 1, 0))
    seq_of = lambda tile_fn: (lambda g: (tile_fn(g)[0], 0, 0))
    rows_block = lambda width, tile_fn: pl.BlockSpec((1, PROMPT_ROWS, width), tile_fn)
    state_block = lambda s, tile_fn: pl.BlockSpec((1,) + s.shape[1:], seq_of(tile_fn))
    sa, sb, sf = _state_shapes(n_seq)
    any_space = pl.BlockSpec(memory_space=pl.ANY)
    matrices = [params[i].shape[1:] for i in MATRIX_PARAMS]
    assert all(k % STAGE_ROWS == 0 for k, _ in matrices)
    outs = pl.pallas_call(
        functools.partial(_prompt_body, tiles_per_seq=tiles_per_seq),
        grid=(n_tiles + 1,),
        in_specs=[rows_block(D_MODEL, mixer_tile), rows_block(D_PLE, ffn_tile)]
                 + [any_space if i in MATRIX_PARAMS else _resident(w.shape)
                    for i, w in enumerate(params)],
        out_specs=(rows_block(D_MODEL, ffn_tile), state_block(sa, mixer_tile),
                   state_block(sb, mixer_tile), state_block(sf, ffn_tile))
                  + (any_space,) * len(matrices),
        out_shape=(jax.ShapeDtypeStruct(x.shape, jnp.float32), sa, sb, sf)
                  + tuple(jax.ShapeDtypeStruct(m, jnp.bfloat16) for m in matrices),
        scratch_shapes=_layer_scratch(1, PROMPT_ROWS)
                       + [pltpu.VMEM(m, jnp.bfloat16) for m in matrices]
                       + [pltpu.VMEM((STAGE_SLOTS, STAGE_ROWS, STAGE_COLS), jnp.float32),
                          pltpu.SemaphoreType.DMA((STAGE_SLOTS,)),
                          pltpu.SemaphoreType.DMA((len(matrices),))],
        compiler_params=pltpu.CompilerParams(
            dimension_semantics=("arbitrary",),
            vmem_limit_bytes=VMEM_LIMIT_BYTES),
        name="prompt_layer",
    )(x, p, *params)
    return outs[:4], outs[4:]


def _run_sample(x, p, states, params):
    n_seq, steps, _ = x.shape
    seqs = SAMPLE_SEQS
    assert n_seq % seqs == 0 and steps % SUBLANES == 0
    n_tiles = n_seq // seqs
    mixer_tile = lambda g: (jnp.minimum(g, n_tiles - 1), 0, 0)
    ffn_tile = lambda g: (jnp.maximum(g - 1, 0), 0, 0)
    block = lambda a, tile_fn: pl.BlockSpec((seqs,) + a.shape[1:], tile_fn)
    rows_major = lambda tile_fn: pl.BlockSpec((K_B - 1, seqs, D_B),
                                              lambda g: (0, tile_fn(g)[0], 0))
    sa_shape, sb_shape, sf_shape = _state_shapes(n_seq)
    sb_rows_major = jax.ShapeDtypeStruct((K_B - 1, n_seq, D_B), jnp.float32)
    state_a, state_b, state_f = states
    y, na, nb, nf = pl.pallas_call(
        functools.partial(_sample_body, seqs=seqs, steps=steps),
        grid=(n_tiles + 1,),
        in_specs=[block(x, mixer_tile), block(p, ffn_tile), block(state_a, mixer_tile),
                  rows_major(mixer_tile), block(state_f, ffn_tile)]
                 + [_resident(w.shape) for w in params],
        out_specs=(block(x, ffn_tile), block(sa_shape, mixer_tile), rows_major(mixer_tile),
                   block(sf_shape, ffn_tile)),
        out_shape=(jax.ShapeDtypeStruct(x.shape, jnp.float32), sa_shape, sb_rows_major, sf_shape),
        scratch_shapes=_layer_scratch(seqs, steps),
        compiler_params=pltpu.CompilerParams(
            dimension_semantics=("arbitrary",),
            vmem_limit_bytes=VMEM_LIMIT_BYTES),
        name="sample_layer",
    )(x, p, state_a, jnp.transpose(state_b, (1, 0, 2)), state_f, *params)
    return y, na, jnp.transpose(nb, (1, 0, 2)), nf


def kernel(x_prompt, x_sample, p_prompt, p_sample, state_conv_a, state_conv_b, state_ffn_conv, g_mix, w_in, conv_a_w, conv_b_w, conv_b_b, ln_b_g, ln_b_b, w_out, g_ffn, w_up, conv_f_w, conv_f_b, w_down, g_ple, w_ple, w_ple_gate, g_final):
    assert g_mix.shape[0] == 1, "single-layer trunk"
    row = lambda v: v.reshape(1, -1)
    params = [row(g_mix[0]), w_in, conv_a_w, conv_b_w, row(conv_b_b[0]),
              row(ln_b_g[0]), row(ln_b_b[0]), w_out, row(g_ffn[0]), w_up,
              conv_f_w, row(conv_f_b[0]), w_down, row(g_ple[0]), w_ple,
              w_ple_gate, row(g_final)]
    (yp, pa, pb, pf), matrices_bf16 = _run_prompt(x_prompt, p_prompt[0], params)
    for i, w in zip(MATRIX_PARAMS, matrices_bf16):
        params[i] = w
    ys, sa, sb, sf = _run_sample(x_sample, p_sample[0],
                                 (state_conv_a[0], state_conv_b[0], state_ffn_conv[0]), params)
    return (yp, ys, pa[None], pb[None], pf[None], sa[None], sb[None], sf[None])
```

```python
import functools

import jax
import jax.numpy as jnp
from jax import lax
from jax.experimental import pallas as pl
from jax.experimental.pallas import tpu as pltpu

D_MODEL = 1024
D_A = 512
D_B = 512
D_FF = 2816
D_PLE = 256
K_A = 3
K_B = 31
K_F = 3
EPS = 1e-6
D_IN = 3 * D_A + 2 * D_B

LANES = 128
SUBLANES = 8
MXU_COLS = 256
VMEM_LIMIT_BYTES = 60 * 1024 * 1024
CONV_GROUP = 8
FFN_CHUNK = 4 * MXU_COLS
HIST_SHORT = SUBLANES
HIST_LONG = -(-(K_B - 1) // SUBLANES) * SUBLANES
PROMPT_ROWS = 256
MATRIX_PARAMS = (1, 7, 9, 12, 14, 15)
STAGE_ROWS, STAGE_COLS, STAGE_SLOTS = 256, 1024, 4
SAMPLE_SEQS = 16


def _rmsnorm(x, g):
    return x * lax.rsqrt(jnp.mean(x * x, axis=-1, keepdims=True) + EPS) * g


def _layernorm(x, g, b):
    mu = jnp.mean(x, axis=-1, keepdims=True)
    xc = x - mu
    var = jnp.mean(xc * xc, axis=-1, keepdims=True)
    return xc * lax.rsqrt(var + EPS) * g + b


def _dot(a, w):
    return jnp.dot(a.astype(jnp.bfloat16), w, preferred_element_type=jnp.float32)


def _store_segments(buf, val, lane_tile0, hist, n_seg, seg_rows):
    pitch = hist + seg_rows
    for jj in range(val.shape[1] // LANES):
        for s in range(n_seg):
            buf[lane_tile0 + jj, s * pitch + hist:s * pitch + hist + seg_rows, :] = (
                val[s * seg_rows:(s + 1) * seg_rows, jj * LANES:(jj + 1) * LANES])


def _conv_short(buf, w_ref, lane_tile0, n_lane_tiles, n_seg, seg_rows):
    pitch = HIST_SHORT + seg_rows
    cols = []
    for jt in range(lane_tile0, lane_tile0 + n_lane_tiles):
        lanes = slice(jt * LANES, (jt + 1) * LANES)
        pieces = []
        for s in range(n_seg):
            first = s * pitch + HIST_SHORT - (K_A - 1)
            pieces.append(sum(w_ref[k:k + 1, lanes] * buf[jt, first + k:first + k + seg_rows, :]
                              for k in range(K_A)))
        cols.append(jnp.concatenate(pieces, axis=0) if n_seg > 1 else pieces[0])
    return jnp.concatenate(cols, axis=1)


def _conv_long(buf, w_ref, out_ref, j, n_seg, seg_rows, always):
    lanes = slice(j * LANES, (j + 1) * LANES)
    pitch = HIST_LONG + seg_rows
    first = HIST_LONG - (K_B - 1)
    n_tiles = seg_rows // SUBLANES
    prev = {}
    for p in range(SUBLANES):
        ks = range(p, K_B, SUBLANES)
        taps = {k: jnp.broadcast_to(w_ref[k:k + 1, lanes], (SUBLANES, LANES)) for k in ks}
        for s in range(n_seg):
            for m0 in range(0, n_tiles, CONV_GROUP):
                tiles = range(m0, min(m0 + CONV_GROUP, n_tiles))
                rows = {m: slice(s * seg_rows + m * SUBLANES, s * seg_rows + (m + 1) * SUBLANES)
                        for m in tiles}
                acc = {m: (out_ref[rows[m], lanes] if p else None) for m in tiles}
                windows = {}
                for m in tiles:
                    for k in ks:
                        start = s * pitch + first + m * SUBLANES + k
                        if start not in windows:
                            windows[start] = buf[j, start:start + SUBLANES, :]
                        term = taps[k] * windows[start]
                        if k == p and (m - m0) in prev:
                            term = jnp.where(always, term, prev[m - m0])
                        acc[m] = term if acc[m] is None else acc[m] + term
                for m in tiles:
                    out_ref[rows[m], lanes] = acc[m]
                    prev[m - m0] = acc[m]


def _layer_halves(get_h, get_p, put_y, params, scratch, *, n_seg, seg_rows, always,
                  mixer=True, ffn=True):
    (g_mix, w_in, conv_a_w, conv_b_w, conv_b_b, ln_b_g, ln_b_b, w_out,
     g_ffn, w_up, conv_f_w, conv_f_b, w_down, g_ple, w_ple, w_ple_gate, g_final) = params
    xa_buf, xb_buf, xf_buf, yb_ref, hmid_ref = scratch

    if ffn:
        h_prev = hmid_ref[...]
        u_ffn = _rmsnorm(h_prev, g_ffn[...]).astype(jnp.bfloat16)

    if mixer:
        h = get_h()
        u_mix = _rmsnorm(h, g_mix[...]).astype(jnp.bfloat16)
        z_b = _dot(u_mix, w_in[:, 3 * D_A:D_IN])
        _store_segments(xb_buf, z_b[:, 0:D_B] * jax.nn.sigmoid(z_b[:, D_B:2 * D_B]),
                        0, HIST_LONG, n_seg, seg_rows)

    ffn_chunks = ([(c, min(c + FFN_CHUNK, D_FF)) for c in range(0, D_FF, FFN_CHUNK)]
                  if ffn else [])
    ffn_acc = None

    def ffn_chunk(c0, c1, acc):
        tile0, n_tiles = c0 // LANES, (c1 - c0) // LANES
        _store_segments(xf_buf, _dot(u_ffn, w_up[:, c0:c1]), tile0, HIST_SHORT, n_seg, seg_rows)
        val = _dot(u_ffn, w_up[:, D_FF + c0:D_FF + c1])
        gate = (_conv_short(xf_buf, conv_f_w, tile0, n_tiles, n_seg, seg_rows)
                + conv_f_b[:, c0:c1])
        part = _dot(jax.nn.silu(gate) * val, w_down[c0:c1, :])
        return part if acc is None else acc + part

    if ffn:
        ffn_acc = ffn_chunk(*ffn_chunks[0], ffn_acc)

    if mixer:
        z_a = _dot(u_mix, w_in[:, 0:3 * D_A])
        _store_segments(xa_buf, z_a[:, D_A:2 * D_A] * z_a[:, 2 * D_A:3 * D_A],
                        0, HIST_SHORT, n_seg, seg_rows)
        ya = z_a[:, 0:D_A] * _conv_short(xa_buf, conv_a_w, 0, D_A // LANES, n_seg, seg_rows)

    n_lane_tiles = D_B // LANES
    rest_chunks = ffn_chunks[1:]
    for j in range(n_lane_tiles):
        if mixer:
            _conv_long(xb_buf, conv_b_w, yb_ref, j, n_seg, seg_rows, always)
        if j < len(rest_chunks):
            ffn_acc = ffn_chunk(*rest_chunks[j], ffn_acc)
    for c0, c1 in rest_chunks[n_lane_tiles:]:
        ffn_acc = ffn_chunk(c0, c1, ffn_acc)

    if ffn:
        h_prev = h_prev + ffn_acc
        gate_p = jax.nn.sigmoid(_dot(_rmsnorm(h_prev, g_ple[...]), w_ple_gate[...]))
        h_prev = h_prev + _dot(get_p(), w_ple[...]) * gate_p
        put_y(_rmsnorm(h_prev, g_final[...]))

    if mixer:
        yb = jax.nn.silu(_layernorm(yb_ref[...] + conv_b_b[...], ln_b_g[...], ln_b_b[...]))
        hmid_ref[...] = h + _dot(jnp.concatenate([ya, yb], axis=-1), w_out[...])


def _history_rows(buf, hist, k, seg, seg_rows, after):
    start = seg * (hist + seg_rows) + hist - (k - 1) + (seg_rows if after else 0)
    return slice(start, start + k - 1)


def _stage_chunks(shapes):
    return [(n, r0, c0, min(STAGE_COLS, cols - c0))
            for n, (rows, cols) in enumerate(shapes)
            for r0 in range(0, rows, STAGE_ROWS)
            for c0 in range(0, cols, STAGE_COLS)]


def _convert_matrices(w_hbm, w_vmem, stage, sem):
    chunks = _stage_chunks([w.shape for w in w_vmem])

    def copy(i):
        n, r0, c0, cols = chunks[i]
        slot = i % STAGE_SLOTS
        return pltpu.make_async_copy(
            w_hbm[n].at[0, pl.ds(r0, STAGE_ROWS), pl.ds(c0, cols)],
            stage.at[slot, :, pl.ds(0, cols)], sem.at[slot])

    ahead = STAGE_SLOTS - 1
    for i in range(min(ahead, len(chunks))):
        copy(i).start()
    for i, (n, r0, c0, cols) in enumerate(chunks):
        if i + ahead < len(chunks):
            copy(i + ahead).start()
        copy(i).wait()
        w_vmem[n][r0:r0 + STAGE_ROWS, c0:c0 + cols] = (
            stage[i % STAGE_SLOTS, :, 0:cols].astype(jnp.bfloat16))


def _prompt_body(x_ref, p_ref, *rest, tiles_per_seq):
    params, rest = list(rest[:17]), rest[17:]
    n_mat = len(MATRIX_PARAMS)
    (y_ref, na_ref, nb_ref, nf_ref), rest = rest[:4], rest[4:]
    w_bf16_hbm, rest = rest[:n_mat], rest[n_mat:]
    (xa_buf, xb_buf, xf_buf, yb_ref, hmid_ref), rest = rest[:5], rest[5:]
    w_vmem, (stage, stage_sem, w_out_sem) = rest[:n_mat], rest[n_mat:]
    w_f32_hbm = [params[i] for i in MATRIX_PARAMS]
    for i, w in zip(MATRIX_PARAMS, w_vmem):
        params[i] = w
    seg_rows = PROMPT_ROWS
    step = pl.program_id(0)
    last = pl.num_programs(0) - 1
    mix_pos = lax.rem(jnp.minimum(step, last - 1), tiles_per_seq)
    ffn_pos = lax.rem(jnp.maximum(step - 1, 0), tiles_per_seq)

    def w_out_copy(n):
        return pltpu.make_async_copy(w_vmem[n], w_bf16_hbm[n], w_out_sem.at[n])

    @pl.when(step == 0)
    def _():
        _convert_matrices(w_f32_hbm, w_vmem, stage, stage_sem)
        for n in range(n_mat):
            w_out_copy(n).start()

    @pl.when(step == last)
    def _():
        for n in range(n_mat):
            w_out_copy(n).wait()

    def put_y(y):
        y_ref[0] = y

    def slide(buf, hist, k):
        new = _history_rows(buf, hist, k, 0, seg_rows, after=True)
        old = _history_rows(buf, hist, k, 0, seg_rows, after=False)
        for j in range(buf.shape[0]):
            buf[j, old, :] = buf[j, new, :]

    always = step <= last

    def run(mixer, ffn):
        if mixer:
            @pl.when(mix_pos == 0)
            def _():
                xa_buf[:, 0:HIST_SHORT, :] = jnp.zeros((D_A // LANES, HIST_SHORT, LANES), jnp.float32)
                xb_buf[:, 0:HIST_LONG, :] = jnp.zeros((D_B // LANES, HIST_LONG, LANES), jnp.float32)
        if ffn:
            @pl.when(ffn_pos == 0)
            def _():
                xf_buf[:, 0:HIST_SHORT, :] = jnp.zeros((D_FF // LANES, HIST_SHORT, LANES), jnp.float32)

        _layer_halves(lambda: x_ref[0], lambda: p_ref[0], put_y, params,
                      (xa_buf, xb_buf, xf_buf, yb_ref, hmid_ref),
                      n_seg=1, seg_rows=seg_rows, always=always, mixer=mixer, ffn=ffn)

        if mixer:
            new_a = _history_rows(xa_buf, HIST_SHORT, K_A, 0, seg_rows, after=True)
            new_b = _history_rows(xb_buf, HIST_LONG, K_B, 0, seg_rows, after=True)

            @pl.when(mix_pos == tiles_per_seq - 1)
            def _():
                for j in range(D_A // LANES):
                    na_ref[0, :, j * LANES:(j + 1) * LANES] = xa_buf[j, new_a, :]
                for j in range(D_B // LANES):
                    nb_ref[0, :, j * LANES:(j + 1) * LANES] = xb_buf[j, new_b, :]

            slide(xa_buf, HIST_SHORT, K_A)
            slide(xb_buf, HIST_LONG, K_B)
        if ffn:
            new_f = _history_rows(xf_buf, HIST_SHORT, K_F, 0, seg_rows, after=True)

            @pl.when(ffn_pos == tiles_per_seq - 1)
            def _():
                for j in range(D_FF // LANES):
                    nf_ref[0, :, j * LANES:(j + 1) * LANES] = xf_buf[j, new_f, :]

            slide(xf_buf, HIST_SHORT, K_F)

    pl.when(step == 0)(lambda: run(mixer=True, ffn=False))
    pl.when((step > 0) & (step < last))(lambda: run(mixer=True, ffn=True))
    pl.when(step == last)(lambda: run(mixer=False, ffn=True))


def _sample_body(x_ref, p_ref, sa_ref, sb_ref, sf_ref, *rest, seqs, steps):
    params, rest = rest[:17], rest[17:]
    (y_ref, na_ref, nb_ref, nf_ref,
     xa_buf, xb_buf, xf_buf, yb_ref, hmid_ref) = rest
    step = pl.program_id(0)
    last = pl.num_programs(0) - 1

    def history(buf, hist, k, seg, after):
        return _history_rows(buf, hist, k, seg, steps, after)

    pitch_b = HIST_LONG + steps
    old_b = history(xb_buf, HIST_LONG, K_B, 0, False).start
    new_b = history(xb_buf, HIST_LONG, K_B, 0, True).start

    def put_y(y):
        for s in range(seqs):
            y_ref[s] = y[s * steps:(s + 1) * steps, :]

    always = step <= last

    def run(mixer, ffn):
        if mixer:
            for s in range(seqs):
                for j in range(D_A // LANES):
                    xa_buf[j, history(xa_buf, HIST_SHORT, K_A, s, False), :] = (
                        sa_ref[s, :, j * LANES:(j + 1) * LANES])
            for k in range(K_B - 1):
                for j in range(D_B // LANES):
                    xb_buf.at[j][pl.ds(old_b + k, seqs, stride=pitch_b), :] = (
                        sb_ref[k, :, j * LANES:(j + 1) * LANES])
        if ffn:
            for s in range(seqs):
                for j in range(D_FF // LANES):
                    xf_buf[j, history(xf_buf, HIST_SHORT, K_F, s, False), :] = (
                        sf_ref[s, :, j * LANES:(j + 1) * LANES])

        _layer_halves(lambda: jnp.concatenate([x_ref[s] for s in range(seqs)], axis=0),
                      lambda: jnp.concatenate([p_ref[s] for s in range(seqs)], axis=0),
                      put_y, params, (xa_buf, xb_buf, xf_buf, yb_ref, hmid_ref),
                      n_seg=seqs, seg_rows=steps, always=always, mixer=mixer, ffn=ffn)

        if mixer:
            for s in range(seqs):
                for j in range(D_A // LANES):
                    na_ref[s, :, j * LANES:(j + 1) * LANES] = (
                        xa_buf[j, history(xa_buf, HIST_SHORT, K_A, s, True), :])
            for k in range(K_B - 1):
                for j in range(D_B // LANES):
                    nb_ref[k, :, j * LANES:(j + 1) * LANES] = (
                        xb_buf.at[j][pl.ds(new_b + k, seqs, stride=pitch_b), :])
        if ffn:
            for s in range(seqs):
                for j in range(D_FF // LANES):
                    nf_ref[s, :, j * LANES:(j + 1) * LANES] = (
                        xf_buf[j, history(xf_buf, HIST_SHORT, K_F, s, True), :])

    pl.when(step == 0)(lambda: run(mixer=True, ffn=False))
    pl.when((step > 0) & (step < last))(lambda: run(mixer=True, ffn=True))
    pl.when(step == last)(lambda: run(mixer=False, ffn=True))


def _resident(shape):
    block = shape if len(shape) == 2 else (None,) + tuple(shape[1:])
    return pl.BlockSpec(block, lambda i: (0,) * len(shape),
                        pipeline_mode=pl.Buffered(1))


def _layer_scratch(n_seg, seg_rows):
    tm = n_seg * seg_rows
    conv_buf = lambda width, hist: pltpu.VMEM(
        (width // LANES, n_seg * (hist + seg_rows), LANES), jnp.float32)
    return [
        conv_buf(D_A, HIST_SHORT),
        conv_buf(D_B, HIST_LONG),
        conv_buf(D_FF, HIST_SHORT),
        pltpu.VMEM((tm, D_B), jnp.float32),
        pltpu.VMEM((tm, D_MODEL), jnp.float32),
    ]


def _state_shapes(n_seq):
    return (jax.ShapeDtypeStruct((n_seq, K_A - 1, D_A), jnp.float32),
            jax.ShapeDtypeStruct((n_seq, K_B - 1, D_B), jnp.float32),
            jax.ShapeDtypeStruct((n_seq, K_F - 1, D_FF), jnp.float32))


def _run_prompt(x, p, params):
    n_seq, t_len, _ = x.shape
    assert t_len % PROMPT_ROWS == 0
    tiles_per_seq = t_len // PROMPT_ROWS
    n_tiles = n_seq * tiles_per_seq

    def tile_at(tile):
        return (tile // tiles_per_seq, lax.rem(tile, tiles_per_seq), 0)

    mixer_tile = lambda g: tile_at(jnp.minimum(g, n_tiles - 1))
    ffn_tile = lambda g: tile_at(jnp.maximum(g - 1, 0))
    seq_of = lambda tile_fn: (lambda g: (tile_fn(g)[0], 0, 0))
    rows_block = lambda width, tile_fn: pl.BlockSpec((1, PROMPT_ROWS, width), tile_fn)
    state_block = lambda s, tile_fn: pl.BlockSpec((1,) + s.shape[1:], seq_of(tile_fn))
    sa, sb, sf = _state_shapes(n_seq)
    any_space = pl.BlockSpec(memory_space=pl.ANY)
    matrices = [params[i].shape[1:] for i in MATRIX_PARAMS]
    assert all(k % STAGE_ROWS == 0 for k, _ in matrices)
    outs = pl.pallas_call(
        functools.partial(_prompt_body, tiles_per_seq=tiles_per_seq),
        grid=(n_tiles + 1,),
        in_specs=[rows_block(D_MODEL, mixer_tile), rows_block(D_PLE, ffn_tile)]
                 + [any_space if i in MATRIX_PARAMS else _resident(w.shape)
                    for i, w in enumerate(params)],
        out_specs=(rows_block(D_MODEL, ffn_tile), state_block(sa, mixer_tile),
                   state_block(sb, mixer_tile), state_block(sf, ffn_tile))
                  + (any_space,) * len(matrices),
        out_shape=(jax.ShapeDtypeStruct(x.shape, jnp.float32), sa, sb, sf)
                  + tuple(jax.ShapeDtypeStruct(m, jnp.bfloat16) for m in matrices),
        scratch_shapes=_layer_scratch(1, PROMPT_ROWS)
                       + [pltpu.VMEM(m, jnp.bfloat16) for m in matrices]
                       + [pltpu.VMEM((STAGE_SLOTS, STAGE_ROWS, STAGE_COLS), jnp.float32),
                          pltpu.SemaphoreType.DMA((STAGE_SLOTS,)),
                          pltpu.SemaphoreType.DMA((len(matrices),))],
        compiler_params=pltpu.CompilerParams(
            dimension_semantics=("arbitrary",),
            vmem_limit_bytes=VMEM_LIMIT_BYTES),
        name="prompt_layer",
    )(x, p, *params)
    return outs[:4], outs[4:]


def _run_sample(x, p, states, params):
    n_seq, steps, _ = x.shape
    seqs = SAMPLE_SEQS
    assert n_seq % seqs == 0 and steps % SUBLANES == 0
    n_tiles = n_seq // seqs
    mixer_tile = lambda g: (jnp.minimum(g, n_tiles - 1), 0, 0)
    ffn_tile = lambda g: (jnp.maximum(g - 1, 0), 0, 0)
    block = lambda a, tile_fn: pl.BlockSpec((seqs,) + a.shape[1:], tile_fn)
    rows_major = lambda tile_fn: pl.BlockSpec((K_B - 1, seqs, D_B),
                                              lambda g: (0, tile_fn(g)[0], 0))
    sa_shape, sb_shape, sf_shape = _state_shapes(n_seq)
    sb_rows_major = jax.ShapeDtypeStruct((K_B - 1, n_seq, D_B), jnp.float32)
    state_a, state_b, state_f = states
    y, na, nb, nf = pl.pallas_call(
        functools.partial(_sample_body, seqs=seqs, steps=steps),
        grid=(n_tiles + 1,),
        in_specs=[block(x, mixer_tile), block(p, ffn_tile), block(state_a, mixer_tile),
                  rows_major(mixer_tile), block(state_f, ffn_tile)]
                 + [_resident(w.shape) for w in params],
        out_specs=(block(x, ffn_tile), block(sa_shape, mixer_tile), rows_major(mixer_tile),
                   block(sf_shape, ffn_tile)),
        out_shape=(jax.ShapeDtypeStruct(x.shape, jnp.float32), sa_shape, sb_rows_major, sf_shape),
        scratch_shapes=_layer_scratch(seqs, steps),
        compiler_params=pltpu.CompilerParams(
            dimension_semantics=("arbitrary",),
            vmem_limit_bytes=VMEM_LIMIT_BYTES),
        name="sample_layer",
    )(x, p, state_a, jnp.transpose(state_b, (1, 0, 2)), state_f, *params)
    return y, na, jnp.transpose(nb, (1, 0, 2)), nf


def kernel(x_prompt, x_sample, p_prompt, p_sample, state_conv_a, state_conv_b, state_ffn_conv, g_mix, w_in, conv_a_w, conv_b_w, conv_b_b, ln_b_g, ln_b_b, w_out, g_ffn, w_up, conv_f_w, conv_f_b, w_down, g_ple, w_ple, w_ple_gate, g_final):
    assert g_mix.shape[0] == 1, "single-layer trunk"
    row = lambda v: v.reshape(1, -1)
    params = [row(g_mix[0]), w_in, conv_a_w, conv_b_w, row(conv_b_b[0]),
              row(ln_b_g[0]), row(ln_b_b[0]), w_out, row(g_ffn[0]), w_up,
              conv_f_w, row(conv_f_b[0]), w_down, row(g_ple[0]), w_ple,
              w_ple_gate, row(g_final)]
    (yp, pa, pb, pf), matrices_bf16 = _run_prompt(x_prompt, p_prompt[0], params)
    for i, w in zip(MATRIX_PARAMS, matrices_bf16):
        params[i] = w
    ys, sa, sb, sf = _run_sample(x_sample, p_sample[0],
                                 (state_conv_a[0], state_conv_b[0], state_ffn_conv[0]), params)
    return (yp, ys, pa[None], pb[None], pf[None], sa[None], sb[None], sf[None])
```
